```python
import jax, jax.numpy as jnp
from jax import lax
import numpy as np

D_MODEL = 1024
BATCH = 1
SEQ = 16384
DEPTH = 4

N_A_LAYERS = DEPTH // 2
N_B_LAYERS = DEPTH - N_A_LAYERS
MEM_TOKENS = 256
MEM_HEADS = 4
MEM_HEAD_DIM = 64
MEM_WIDTH = MEM_HEADS * MEM_HEAD_DIM
CONV_CH = D_MODEL - MEM_WIDTH
CONV_WIDTH = 31
MIX_WIDTH_A = CONV_CH + MEM_WIDTH
SB_HEADS = 4
SB_HEAD_DIM = 128
SB_WIDTH = SB_HEADS * SB_HEAD_DIM
MIX_WIDTH_B = SB_WIDTH + MEM_WIDTH
Q_BLOCK = 128
N_GROUPS = 4
EXPERTS_PER_GROUP = 4
EXPERT_TOP_K = 2
EXPERT_HIDDEN = D_MODEL // 8
EPS = 1e-6

kernel_name = "yoco_conformer_stickbreak_hmoe"


def rms_norm(x, g):
    xf = x.astype(jnp.float32)
    y = xf * lax.rsqrt(jnp.mean(xf * xf, axis=-1, keepdims=True) + EPS)
    return (y * g.astype(jnp.float32)).astype(x.dtype)


def layer_norm(x, g, b):
    xf = x.astype(jnp.float32)
    xc = xf - jnp.mean(xf, axis=-1, keepdims=True)
    var = jnp.mean(xc * xc, axis=-1, keepdims=True)
    y = xc * lax.rsqrt(var + EPS) * g.astype(jnp.float32) + b.astype(jnp.float32)
    return y.astype(x.dtype)


def split_heads(x, n_heads, head_dim):
    b, s, _ = x.shape
    return x.reshape(b, s, n_heads, head_dim).transpose(0, 2, 1, 3)


def merge_heads(x):
    b, h, s, d = x.shape
    return x.transpose(0, 2, 1, 3).reshape(b, s, h * d)


def conformer_conv(u, conv_w, ln_g, ln_b):
    a, gate = jnp.split(u, 2, axis=-1)
    c = a * jax.nn.sigmoid(gate)
    c = lax.conv_general_dilated(
        c, conv_w.astype(c.dtype)[:, None, :], window_strides=(1,),
        padding=[(CONV_WIDTH - 1, 0)], dimension_numbers=("NWC", "WIO", "NWC"),
        feature_group_count=CONV_CH)
    c = layer_norm(c, ln_g, ln_b)
    return jax.nn.silu(c)


def memory_attention(q, mem_k, mem_v, q_g):
    qh = rms_norm(split_heads(q, MEM_HEADS, MEM_HEAD_DIM), q_g).astype(jnp.float32)
    s = jnp.einsum("bhsd,bhmd->bhsm", qh, mem_k.astype(jnp.float32)) * (MEM_HEAD_DIM ** -0.5)
    p = jax.nn.softmax(s, axis=-1)
    o = jnp.einsum("bhsm,bhmd->bhsd", p, mem_v.astype(jnp.float32))
    return merge_heads(o).astype(q.dtype)


def stick_breaking_attention(q, k, v):
    b, h, seq, _ = q.shape
    n_blocks = seq // Q_BLOCK
    scale = SB_HEAD_DIM ** -0.5
    idx = jnp.arange(Q_BLOCK)
    diag_mask = idx[None, :] < idx[:, None]
    u_rev = jnp.asarray(np.tri(Q_BLOCK, Q_BLOCK, 0, dtype=np.float32))
    kf = k.astype(jnp.float32)
    vf = v.astype(jnp.float32)
    outs = []
    for i in range(n_blocks):
        n_k = i + 1
        n_keys = n_k * Q_BLOCK
        qb = q[:, :, i * Q_BLOCK:(i + 1) * Q_BLOCK].astype(jnp.float32)
        z = jnp.einsum("bhqd,bhkd->bhqk", qb, kf[:, :, :n_keys]) * scale
        mask = jnp.concatenate(
            [jnp.ones((Q_BLOCK, n_keys - Q_BLOCK), bool), diag_mask], axis=1)
        lf = jnp.where(mask, jax.nn.log_sigmoid(-z), 0.0)
        lf = lf.reshape(b, h, Q_BLOCK, n_k, Q_BLOCK)
        within = jnp.einsum("bhqnk,kj->bhqnj", lf, u_rev)
        later_tri = jnp.asarray(np.tri(n_k, n_k, -1, dtype=np.float32))
        later = jnp.einsum("bhqn,nm->bhqm", jnp.sum(lf, axis=-1), later_tri)
        rev = (within + later[..., None]).reshape(b, h, Q_BLOCK, n_keys)
        w = jnp.exp(jnp.where(mask, z + rev, -jnp.inf))
        outs.append(jnp.einsum("bhqk,bhkd->bhqd", w, vf[:, :, :n_keys]))
    out = jnp.concatenate(outs, axis=2)
    return merge_heads(out).astype(q.dtype)


def hier_moe(h, rg_w, rg_b, re_w, re_b, w_in, w_down):
    b, s, d = h.shape
    t = h.reshape(-1, d)
    tf = t.astype(jnp.float32)
    group_logits = tf @ rg_w.astype(jnp.float32) + rg_b.astype(jnp.float32)
    group_prob = jax.nn.softmax(group_logits, axis=-1)
    g_idx = jnp.argmax(group_logits, axis=-1)
    g_onehot = jax.nn.one_hot(g_idx, N_GROUPS, dtype=jnp.float32)
    g_gate = jnp.max(group_prob, axis=-1, keepdims=True)
    exp_logits = jnp.einsum("td,gde->tge", tf, re_w.astype(jnp.float32)) + re_b.astype(jnp.float32)
    sel_logits = jnp.einsum("tge,tg->te", exp_logits, g_onehot)
    top_v, top_i = lax.top_k(sel_logits, EXPERT_TOP_K)
    top_w = jax.nn.softmax(top_v, axis=-1) * g_gate
    expert_gate = jnp.sum(jax.nn.one_hot(top_i, EXPERTS_PER_GROUP, dtype=jnp.float32)
                          * top_w[..., None], axis=1)
    gate = g_onehot[:, :, None] * expert_gate[:, None, :]
    y = jnp.zeros(t.shape, jnp.float32)
    for g in range(N_GROUPS):
        hu = jnp.einsum("td,edf->tef", t, w_in[g])
        a, u = jnp.split(hu, 2, axis=-1)
        act = jax.nn.silu(a) * u * gate[:, g, :, None].astype(hu.dtype)
        y = y + jnp.einsum("teh,ehd->td", act, w_down[g]).astype(jnp.float32)
    return y.reshape(b, s, d).astype(h.dtype)


def setup_inputs(seed: int = 0) -> dict:
    key = jax.random.key(seed)
    ks = jax.random.split(key, 25)
    f32 = jnp.float32

    def nrm(k, shape, scale):
        return jax.random.normal(k, shape, f32) * scale

    def gain(k, shape):
        return 1.0 + 0.02 * jax.random.normal(k, shape, f32)

    n_a, n_b = N_A_LAYERS, N_B_LAYERS
    return {
        "x": nrm(ks[0], (BATCH, SEQ, D_MODEL), 1.0),
        "mem": nrm(ks[1], (BATCH, MEM_TOKENS, D_MODEL), 1.0),
        "mem_norm_g": gain(ks[2], (D_MODEL,)),
        "mix_norm_g": gain(ks[3], (DEPTH, D_MODEL)),
        "ffn_norm_g": gain(ks[4], (DEPTH, D_MODEL)),
        "w_in_a": nrm(ks[5], (n_a, D_MODEL, 2 * CONV_CH + MEM_WIDTH), D_MODEL ** -0.5),
        "conv_w": nrm(ks[6], (n_a, CONV_WIDTH, CONV_CH), CONV_WIDTH ** -0.5),
        "conv_ln_g": gain(ks[7], (n_a, CONV_CH)),
        "conv_ln_b": nrm(ks[8], (n_a, CONV_CH), 0.02),
        "w_out_a": nrm(ks[9], (n_a, MIX_WIDTH_A, D_MODEL), 0.5 * MIX_WIDTH_A ** -0.5),
        "w_in_b": nrm(ks[10], (n_b, D_MODEL, SB_WIDTH + MEM_WIDTH), D_MODEL ** -0.5),
        "sb_q_norm_g": gain(ks[11], (n_b, SB_HEAD_DIM)),
        "w_out_b": nrm(ks[12], (n_b, MIX_WIDTH_B, D_MODEL), 0.5 * MIX_WIDTH_B ** -0.5),
        "kv_norm_g": gain(ks[13], (D_MODEL,)),
        "w_kv": nrm(ks[14], (D_MODEL, 2 * SB_WIDTH), D_MODEL ** -0.5),
        "sb_k_norm_g": gain(ks[15], (SB_HEAD_DIM,)),
        "w_mem_kv": nrm(ks[16], (DEPTH, D_MODEL, 2 * MEM_WIDTH), D_MODEL ** -0.5),
        "mem_q_norm_g": gain(ks[17], (DEPTH, MEM_HEAD_DIM)),
        "mem_k_norm_g": gain(ks[18], (DEPTH, MEM_HEAD_DIM)),
        "router_g_w": nrm(ks[19], (DEPTH, D_MODEL, N_GROUPS), D_MODEL ** -0.5),
        "router_g_b": nrm(ks[20], (DEPTH, N_GROUPS), 0.01),
        "router_e_w": nrm(ks[21], (DEPTH, N_GROUPS, D_MODEL, EXPERTS_PER_GROUP), D_MODEL ** -0.5),
        "router_e_b": nrm(ks[22], (DEPTH, N_GROUPS, EXPERTS_PER_GROUP), 0.01),
        "moe_w_in": nrm(ks[23], (DEPTH, N_GROUPS, EXPERTS_PER_GROUP, D_MODEL, 2 * EXPERT_HIDDEN), D_MODEL ** -0.5),
        "moe_w_down": nrm(ks[24], (DEPTH, N_GROUPS, EXPERTS_PER_GROUP, EXPERT_HIDDEN, D_MODEL), EXPERT_HIDDEN ** -0.5),
    }


def reference(x, mem, mem_norm_g, mix_norm_g, ffn_norm_g, w_in_a, conv_w, conv_ln_g, conv_ln_b,
              w_out_a, w_in_b, sb_q_norm_g, w_out_b, kv_norm_g, w_kv, sb_k_norm_g, w_mem_kv,
              mem_q_norm_g, mem_k_norm_g, router_g_w, router_g_b, router_e_w, router_e_b,
              moe_w_in, moe_w_down):
    mem_n = rms_norm(mem, mem_norm_g)
    sb_k = None
    sb_v = None
    for l in range(DEPTH):
        h = rms_norm(x, mix_norm_g[l])
        mk, mv = jnp.split(mem_n @ w_mem_kv[l], 2, axis=-1)
        mk = rms_norm(split_heads(mk, MEM_HEADS, MEM_HEAD_DIM), mem_k_norm_g[l])
        mv = split_heads(mv, MEM_HEADS, MEM_HEAD_DIM)
        if l < N_A_LAYERS:
            u = h @ w_in_a[l]
            glu_in, q_mem = u[..., :2 * CONV_CH], u[..., 2 * CONV_CH:]
            mixed = conformer_conv(glu_in, conv_w[l], conv_ln_g[l], conv_ln_b[l])
            w_out = w_out_a[l]
        else:
            j = l - N_A_LAYERS
            u = h @ w_in_b[j]
            q_sb, q_mem = u[..., :SB_WIDTH], u[..., SB_WIDTH:]
            q_sb = rms_norm(split_heads(q_sb, SB_HEADS, SB_HEAD_DIM), sb_q_norm_g[j])
            mixed = stick_breaking_attention(q_sb, sb_k, sb_v)
            w_out = w_out_b[j]
        mem_out = memory_attention(q_mem, mk, mv, mem_q_norm_g[l])
        x = x + jnp.concatenate([mixed.astype(x.dtype), mem_out.astype(x.dtype)], axis=-1) @ w_out
        x = x + hier_moe(rms_norm(x, ffn_norm_g[l]), router_g_w[l], router_g_b[l],
                         router_e_w[l], router_e_b[l], moe_w_in[l], moe_w_down[l])
        if l == N_A_LAYERS - 1:
            s = rms_norm(x, kv_norm_g)
            k_all, v_all = jnp.split(s @ w_kv, 2, axis=-1)
            sb_k = rms_norm(split_heads(k_all, SB_HEADS, SB_HEAD_DIM), sb_k_norm_g)
            sb_v = split_heads(v_all, SB_HEADS, SB_HEAD_DIM)
    return x
```

```python
import functools

import jax
import jax.numpy as jnp
from jax import lax
from jax.experimental import pallas as pl
from jax.experimental.pallas import tpu as pltpu

F32 = jnp.float32
BF16 = jnp.bfloat16

D_MODEL = 1024
DEPTH = 4
N_A_LAYERS = DEPTH // 2
MEM_TOKENS = 256
MEM_HEADS = 4
MEM_HEAD_DIM = 64
MEM_WIDTH = MEM_HEADS * MEM_HEAD_DIM
CONV_CH = D_MODEL - MEM_WIDTH
CONV_WIDTH = 31
SB_HEADS = 4
SB_HEAD_DIM = 128
SB_WIDTH = SB_HEADS * SB_HEAD_DIM
N_GROUPS = 4
EXPERTS_PER_GROUP = 4
N_EXPERTS = N_GROUPS * EXPERTS_PER_GROUP
EXPERT_HIDDEN = D_MODEL // 8
EPS = 1e-6

LANES = 128
SUBLANES = 8
CONV_HALO = 32
CONV_ROWS = 64
CONV_PAD = 2 * SUBLANES
SB_BLOCK = 128
SB_ZERO_LOG = -106.0
VMEM_LIMIT = 56 * 1024 * 1024


def _dot(a, b):
    return jnp.dot(a, b, preferred_element_type=F32)


def _split(a):
    hi = a.astype(BF16)
    lo = (a - hi.astype(F32)).astype(BF16)
    return hi, lo


def _dot_exact_rhs(a, b):
    hi, lo = _split(a)
    return _dot(hi, b) + _dot(lo, b)


def _dot3(a, b):
    ah, al = _split(a)
    bh, bl = _split(b)
    return _dot(ah, bh) + (_dot(ah, bl) + _dot(al, bh))


def _rms(x, g):
    ms = jnp.mean(x * x, axis=-1, keepdims=True)
    return x * lax.rsqrt(ms + EPS) * g


def _sigmoid(x):
    return 1.0 / (1.0 + jnp.exp(-x))


def _head_block_ones(n, head_dim):
    r = lax.broadcasted_iota(jnp.int32, (n, n), 0) // head_dim
    c = lax.broadcasted_iota(jnp.int32, (n, n), 1) // head_dim
    return (r == c).astype(BF16)


def _memkv_kernel(mem_ref, g_ref, w_ref, kg_ref, mk_ref, mv_ref):
    mem_n = _rms(mem_ref[...], g_ref[...]).astype(BF16)
    kv = _dot(mem_n, w_ref[...])
    mk = kv[:, :MEM_WIDTH]
    mv = kv[:, MEM_WIDTH:]
    ss = _dot_exact_rhs(mk * mk, _head_block_ones(MEM_WIDTH, MEM_HEAD_DIM))
    mk = mk * lax.rsqrt(ss * (1.0 / MEM_HEAD_DIM) + EPS) * (kg_ref[...] * MEM_HEAD_DIM ** -0.5)
    mk_t = mk.T
    mk_ref[...] = jnp.zeros(mk_ref.shape, mk_ref.dtype)
    lane_head = lax.broadcasted_iota(jnp.int32, mv.shape, 1) // MEM_HEAD_DIM
    for h in range(MEM_HEADS):
        rows = slice(h * MEM_HEAD_DIM, (h + 1) * MEM_HEAD_DIM)
        mk_ref[rows, h * MEM_TOKENS:(h + 1) * MEM_TOKENS] = mk_t[rows, :].astype(BF16)
        mv_ref[h * MEM_TOKENS:(h + 1) * MEM_TOKENS, :] = jnp.where(lane_head == h, mv, 0.0).astype(BF16)


def _memkv(mem, mem_norm_g, w_mem_kv, mem_k_norm_g):
    kg = jnp.tile(mem_k_norm_g, (1, MEM_HEADS)).reshape(DEPTH, 1, MEM_WIDTH)
    return pl.pallas_call(
        _memkv_kernel,
        grid=(DEPTH,),
        in_specs=[
            pl.BlockSpec((MEM_TOKENS, D_MODEL), lambda l: (0, 0)),
            pl.BlockSpec((1, D_MODEL), lambda l: (0, 0)),
            pl.BlockSpec((None, D_MODEL, 2 * MEM_WIDTH), lambda l: (l, 0, 0)),
            pl.BlockSpec((None, 1, MEM_WIDTH), lambda l: (l, 0, 0)),
        ],
        out_specs=[
            pl.BlockSpec((None, MEM_WIDTH, MEM_HEADS * MEM_TOKENS), lambda l: (l, 0, 0)),
            pl.BlockSpec((None, MEM_HEADS * MEM_TOKENS, MEM_WIDTH), lambda l: (l, 0, 0)),
        ],
        out_shape=[
            jax.ShapeDtypeStruct((DEPTH, MEM_WIDTH, MEM_HEADS * MEM_TOKENS), BF16),
            jax.ShapeDtypeStruct((DEPTH, MEM_HEADS * MEM_TOKENS, MEM_WIDTH), BF16),
        ],
        name="memkv",
    )(mem, mem_norm_g.reshape(1, D_MODEL), w_mem_kv.astype(BF16), kg)


def _mem_attention(q, qg, mk_big, mv_big):
    ss = _dot_exact_rhs(q * q, _head_block_ones(MEM_WIDTH, MEM_HEAD_DIM))
    qn = q * lax.rsqrt(ss * (1.0 / MEM_HEAD_DIM) + EPS) * qg
    s = _dot(qn.astype(BF16), mk_big)
    probs = []
    for h in range(MEM_HEADS):
        sh = s[:, h * MEM_TOKENS:(h + 1) * MEM_TOKENS]
        e = jnp.exp(sh - jnp.max(sh, axis=-1, keepdims=True))
        probs.append((e * (1.0 / jnp.sum(e, axis=-1, keepdims=True))).astype(BF16))
    return _dot(jnp.concatenate(probs, axis=1), mv_big)


def _mixer_a_kernel(x_ref, g_ref, win_ref, cw_ref, lng_ref, lnb_ref, qg_ref, mk_ref, mv_ref,
                    wout_ref, o_ref, cbuf_ref, conv_ref):
    tm = x_ref.shape[0]

    @pl.when(pl.program_id(0) == 0)
    def _():
        cbuf_ref[0:CONV_HALO, :] = jnp.zeros((CONV_HALO, CONV_CH), F32)
        cbuf_ref[CONV_HALO + tm:, :] = jnp.zeros((CONV_PAD, CONV_CH), F32)

    x = x_ref[...]
    h = _rms(x, g_ref[...]).astype(BF16)
    u = _dot(h, win_ref[...])
    cbuf_ref[CONV_HALO:CONV_HALO + tm, :] = u[:, :CONV_CH] * _sigmoid(u[:, CONV_CH:2 * CONV_CH])

    base = CONV_HALO - (CONV_WIDTH - 1)
    win = CONV_ROWS + 2 * SUBLANES

    def conv_chunk(r, carry):
        r0 = pl.multiple_of(r * CONV_ROWS, CONV_ROWS)
        for c0 in range(0, CONV_CH, LANES):
            cols = slice(c0, c0 + LANES)
            acc = jnp.zeros((CONV_ROWS, LANES), F32)
            for s in range(SUBLANES):
                part = jnp.zeros((win, LANES), F32)
                for k in range(s, CONV_WIDTH, SUBLANES):
                    start = pl.multiple_of(r0 + (k - s), SUBLANES)
                    part = part + cw_ref[k:k + 1, cols] * cbuf_ref[pl.ds(start, win), cols]
                acc = acc + part[base + s:base + s + CONV_ROWS, :]
            conv_ref[pl.ds(r0, CONV_ROWS), cols] = acc
        return carry

    lax.fori_loop(0, tm // CONV_ROWS, conv_chunk, 0)
    cbuf_ref[0:CONV_HALO, :] = cbuf_ref[tm:tm + CONV_HALO, :]

    c = conv_ref[...]
    xc = c - jnp.mean(c, axis=-1, keepdims=True)
    var = jnp.mean(xc * xc, axis=-1, keepdims=True)
    y = xc * lax.rsqrt(var + EPS) * lng_ref[...] + lnb_ref[...]
    mixed = (y * _sigmoid(y)).astype(BF16)
    mem_out = _mem_attention(u[:, 2 * CONV_CH:], qg_ref[...], mk_ref[...], mv_ref[...]).astype(BF16)
    o_ref[...] = x + _dot(mixed, wout_ref[0:CONV_CH, :]) + _dot(mem_out, wout_ref[CONV_CH:, :])


def _mixer_a(x, g, w_in, conv_w, ln_g, ln_b, qg, mk_big, mv_big, w_out, tm):
    t = x.shape[0]
    full = lambda shape: pl.BlockSpec(shape, lambda i: (0,) * len(shape))
    return pl.pallas_call(
        _mixer_a_kernel,
        grid=(t // tm,),
        in_specs=[
            pl.BlockSpec((tm, D_MODEL), lambda i: (i, 0)),
            full((1, D_MODEL)),
            full(w_in.shape),
            full(conv_w.shape),
            full((1, CONV_CH)),
            full((1, CONV_CH)),
            full((1, MEM_WIDTH)),
            full(mk_big.shape),
            full(mv_big.shape),
            full(w_out.shape),
        ],
        out_specs=pl.BlockSpec((tm, D_MODEL), lambda i: (i, 0)),
        out_shape=jax.ShapeDtypeStruct((t, D_MODEL), F32),
        scratch_shapes=[
            pltpu.VMEM((tm + CONV_HALO + CONV_PAD, CONV_CH), F32),
            pltpu.VMEM((tm, CONV_CH), F32),
        ],
        compiler_params=pltpu.CompilerParams(
            dimension_semantics=("arbitrary",), vmem_limit_bytes=VMEM_LIMIT),
        name="mixer_a",
    )(x, g, w_in, conv_w, ln_g, ln_b, qg, mk_big, mv_big, w_out)


ROUTER_LANES = LANES
EXPERT_LANE0 = N_GROUPS


def _route(logits):
    neg = jnp.float32(-jnp.inf)
    lane = lax.broadcasted_iota(jnp.int32, logits.shape, 1)
    lane_f = lane.astype(F32)
    is_g = lane < N_GROUPS
    gl = jnp.where(is_g, logits, neg)
    gmax = jnp.max(gl, axis=-1, keepdims=True)
    gidx = jnp.min(jnp.where(gl == gmax, lane_f, float(ROUTER_LANES)), axis=-1, keepdims=True)
    gsum = jnp.sum(jnp.where(is_g, jnp.exp(gl - gmax), 0.0), axis=-1, keepdims=True)
    g_gate = 1.0 / gsum
    lane_group = ((lane - EXPERT_LANE0) // EXPERTS_PER_GROUP).astype(F32)
    sel = (lane >= EXPERT_LANE0) & (lane < EXPERT_LANE0 + N_EXPERTS) & (lane_group == gidx)
    sl = jnp.where(sel, logits, neg)
    m1 = jnp.max(sl, axis=-1, keepdims=True)
    i1 = jnp.min(jnp.where(sl == m1, lane_f, float(ROUTER_LANES)), axis=-1, keepdims=True)
    sl2 = jnp.where(lane_f == i1, neg, sl)
    m2 = jnp.max(sl2, axis=-1, keepdims=True)
    i2 = jnp.min(jnp.where(sl2 == m2, lane_f, float(ROUTER_LANES)), axis=-1, keepdims=True)
    e2 = jnp.exp(m2 - m1)
    w1 = g_gate / (1.0 + e2)
    w2 = w1 * e2
    return jnp.where(lane_f == i1, w1, jnp.where(lane_f == i2, w2, 0.0))


def _moe_kernel(x_ref, g_ref, wr_ref, br_ref, win_ref, wdn_ref, o_ref, act_ref):
    x = x_ref[...]
    hf = _rms(x, g_ref[...])
    hb = hf.astype(BF16)
    gate = _route(_dot3(hf, wr_ref[...]) + br_ref[...])
    for e in range(N_EXPERTS):
        hu = _dot(hb, win_ref[e])
        a = hu[:, :EXPERT_HIDDEN]
        act = a * _sigmoid(a) * hu[:, EXPERT_HIDDEN:] * gate[:, EXPERT_LANE0 + e:EXPERT_LANE0 + e + 1]
        act_ref[:, e * EXPERT_HIDDEN:(e + 1) * EXPERT_HIDDEN] = act.astype(BF16)
    o_ref[...] = x + _dot(act_ref[...], wdn_ref[...])


def _moe(x, g, w_router, b_router, w_in, w_down, tm):
    t = x.shape[0]
    full = lambda shape: pl.BlockSpec(shape, lambda i: (0,) * len(shape))
    return pl.pallas_call(
        _moe_kernel,
        grid=(t // tm,),
        in_specs=[
            pl.BlockSpec((tm, D_MODEL), lambda i: (i, 0)),
            full((1, D_MODEL)),
            full(w_router.shape),
            full(b_router.shape),
            full(w_in.shape),
            full(w_down.shape),
        ],
        out_specs=pl.BlockSpec((tm, D_MODEL), lambda i: (i, 0)),
        out_shape=jax.ShapeDtypeStruct((t, D_MODEL), F32),
        scratch_shapes=[pltpu.VMEM((tm, N_EXPERTS * EXPERT_HIDDEN), BF16)],
        compiler_params=pltpu.CompilerParams(
            dimension_semantics=("arbitrary",), vmem_limit_bytes=VMEM_LIMIT),
        name="moe",
    )(x, g, w_router, b_router, w_in, w_down)


def _kv_kernel(x_ref, g_ref, w_ref, kg_ref, k_ref, v_ref):
    s = _rms(x_ref[...], g_ref[...]).astype(BF16)
    kv = _dot(s, w_ref[...])
    for h in range(SB_HEADS):
        cols = slice(h * SB_HEAD_DIM, (h + 1) * SB_HEAD_DIM)
        k_ref[:, cols] = _rms(kv[:, cols], kg_ref[...]).astype(BF16)
    v_ref[...] = kv[:, SB_WIDTH:].astype(BF16)


def _kv_proj(x, g, w_kv, kg, tm):
    t = x.shape[0]
    full = lambda shape: pl.BlockSpec(shape, lambda i: (0,) * len(shape))
    return pl.pallas_call(
        _kv_kernel,
        grid=(t // tm,),
        in_specs=[
            pl.BlockSpec((tm, D_MODEL), lambda i: (i, 0)),
            full((1, D_MODEL)),
            full(w_kv.shape),
            full((1, SB_HEAD_DIM)),
        ],
        out_specs=[pl.BlockSpec((tm, SB_WIDTH), lambda i: (i, 0))] * 2,
        out_shape=[jax.ShapeDtypeStruct((t, SB_WIDTH), BF16)] * 2,
        compiler_params=pltpu.CompilerParams(
            dimension_semantics=("arbitrary",), vmem_limit_bytes=VMEM_LIMIT),
        name="kv_proj",
    )(x, g, w_kv, kg)


def _mixer_b_pre_kernel(x_ref, g_ref, win_ref, sqg_ref, qg_ref, mk_ref, mv_ref, q_ref, mo_ref):
    h = _rms(x_ref[...], g_ref[...]).astype(BF16)
    u = _dot(h, win_ref[...])
    sqg = sqg_ref[...] * SB_HEAD_DIM ** -0.5
    for hd in range(SB_HEADS):
        cols = slice(hd * SB_HEAD_DIM, (hd + 1) * SB_HEAD_DIM)
        q_ref[:, cols] = _rms(u[:, cols], sqg).astype(BF16)
    mo_ref[...] = _mem_attention(u[:, SB_WIDTH:], qg_ref[...], mk_ref[...], mv_ref[...]).astype(BF16)


def _mixer_b_pre(x, g, w_in, sqg, qg, mk_big, mv_big, tm):
    t = x.shape[0]
    full = lambda shape: pl.BlockSpec(shape, lambda i: (0,) * len(shape))
    return pl.pallas_call(
        _mixer_b_pre_kernel,
        grid=(t // tm,),
        in_specs=[
            pl.BlockSpec((tm, D_MODEL), lambda i: (i, 0)),
            full((1, D_MODEL)),
            full(w_in.shape),
            full((1, SB_HEAD_DIM)),
            full((1, MEM_WIDTH)),
            full(mk_big.shape),
            full(mv_big.shape),
        ],
        out_specs=[pl.BlockSpec((tm, SB_WIDTH), lambda i: (i, 0)),
                   pl.BlockSpec((tm, MEM_WIDTH), lambda i: (i, 0))],
        out_shape=[jax.ShapeDtypeStruct((t, SB_WIDTH), BF16),
                   jax.ShapeDtypeStruct((t, MEM_WIDTH), BF16)],
        compiler_params=pltpu.CompilerParams(
            dimension_semantics=("arbitrary",), vmem_limit_bytes=VMEM_LIMIT),
        name="mixer_b_pre",
    )(x, g, w_in, sqg, qg, mk_big, mv_big)


def _sb_kernel(q_ref, k_ref, v_ref, o_ref):
    i = pl.program_id(1)
    q = q_ref[...]
    row = lax.broadcasted_iota(jnp.int32, (SB_BLOCK, SB_BLOCK), 0)
    col = lax.broadcasted_iota(jnp.int32, (SB_BLOCK, SB_BLOCK), 1)
    causal = col < row
    suffix_ones = (row >= col).astype(BF16)

    def key_block(kb, carry, acc, diagonal):
        start = pl.multiple_of(kb * SB_BLOCK, SB_BLOCK)
        k = k_ref[pl.ds(start, SB_BLOCK), :]
        v = v_ref[pl.ds(start, SB_BLOCK), :]
        z = lax.dot_general(q, k, (((1,), (1,)), ((), ())), preferred_element_type=F32)
        lf = jnp.minimum(-z, 0.0) - jnp.log1p(jnp.exp(-jnp.abs(z)))
        if diagonal:
            lf = jnp.where(causal, lf, 0.0)
        within = _dot_exact_rhs(lf, suffix_ones)
        w = jnp.exp(z + within + carry)
        if diagonal:
            w = jnp.where(causal, w, 0.0)
        acc = acc + _dot(w.astype(BF16), v)
        return carry + within[:, 0:1], acc

    carry, acc = key_block(i, jnp.zeros((SB_BLOCK, 1), F32),
                           jnp.zeros((SB_BLOCK, SB_HEAD_DIM), F32), True)

    def cond(state):
        kb, carry, _ = state
        return jnp.logical_and(kb >= 0, jnp.max(carry) > SB_ZERO_LOG)

    def body(state):
        kb, carry, acc = state
        carry, acc = key_block(kb, carry, acc, False)
        return kb - 1, carry, acc

    _, _, acc = lax.while_loop(cond, body, (i - 1, carry, acc))
    o_ref[...] = acc.astype(o_ref.dtype)


def _sb_attention(q, k, v):
    t = q.shape[0]
    return pl.pallas_call(
        _sb_kernel,
        grid=(SB_HEADS, t // SB_BLOCK),
        in_specs=[
            pl.BlockSpec((SB_BLOCK, SB_HEAD_DIM), lambda h, i: (i, h)),
            pl.BlockSpec((t, SB_HEAD_DIM), lambda h, i: (0, h)),
            pl.BlockSpec((t, SB_HEAD_DIM), lambda h, i: (0, h)),
        ],
        out_specs=pl.BlockSpec((SB_BLOCK, SB_HEAD_DIM), lambda h, i: (i, h)),
        out_shape=jax.ShapeDtypeStruct((t, SB_WIDTH), BF16),
        compiler_params=pltpu.CompilerParams(
            dimension_semantics=("arbitrary", "arbitrary"), vmem_limit_bytes=VMEM_LIMIT),
        name="sb_attention",
    )(q, k, v)


def _out_b_kernel(x_ref, sb_ref, mo_ref, wout_ref, o_ref):
    o_ref[...] = (x_ref[...] + _dot(sb_ref[...], wout_ref[0:SB_WIDTH, :])
                  + _dot(mo_ref[...], wout_ref[SB_WIDTH:, :]))


def _out_b(x, sb, mo, w_out, tm):
    t = x.shape[0]
    return pl.pallas_call(
        _out_b_kernel,
        grid=(t // tm,),
        in_specs=[
            pl.BlockSpec((tm, D_MODEL), lambda i: (i, 0)),
            pl.BlockSpec((tm, SB_WIDTH), lambda i: (i, 0)),
            pl.BlockSpec((tm, MEM_WIDTH), lambda i: (i, 0)),
            pl.BlockSpec(w_out.shape, lambda i: (0, 0)),
        ],
        out_specs=pl.BlockSpec((tm, D_MODEL), lambda i: (i, 0)),
        out_shape=jax.ShapeDtypeStruct((t, D_MODEL), F32),
        compiler_params=pltpu.CompilerParams(
            dimension_semantics=("arbitrary",), vmem_limit_bytes=VMEM_LIMIT),
        name="out_b",
    )(x, sb, mo, w_out)


def _token_tile(t):
    tm = 512
    while t % tm:
        tm //= 2
    return tm


def kernel(x, mem, mem_norm_g, mix_norm_g, ffn_norm_g, w_in_a, conv_w, conv_ln_g, conv_ln_b, w_out_a, w_in_b, sb_q_norm_g, w_out_b, kv_norm_g, w_kv, sb_k_norm_g, w_mem_kv, mem_q_norm_g, mem_k_norm_g, router_g_w, router_g_b, router_e_w, router_e_b, moe_w_in, moe_w_down):
    b, t, d = x.shape
    assert b == 1 and d == D_MODEL and t % SB_BLOCK == 0
    tm = _token_tile(t)
    xt = x.reshape(t, d)

    mk_big, mv_big = _memkv(mem.reshape(MEM_TOKENS, d), mem_norm_g, w_mem_kv, mem_k_norm_g)
    mem_qg = jnp.tile(mem_q_norm_g, (1, MEM_HEADS)).reshape(DEPTH, 1, MEM_WIDTH)

    pad = ROUTER_LANES - N_GROUPS - N_EXPERTS
    w_router = jnp.concatenate(
        [router_g_w, router_e_w.transpose(0, 2, 1, 3).reshape(DEPTH, d, N_EXPERTS),
         jnp.zeros((DEPTH, d, pad), F32)], axis=-1)
    b_router = jnp.concatenate(
        [router_g_b, router_e_b.reshape(DEPTH, N_EXPERTS), jnp.zeros((DEPTH, pad), F32)],
        axis=-1).reshape(DEPTH, 1, ROUTER_LANES)
    moe_in = moe_w_in.astype(BF16).reshape(DEPTH, N_EXPERTS, d, 2 * EXPERT_HIDDEN)
    moe_down = moe_w_down.astype(BF16).reshape(DEPTH, N_EXPERTS * EXPERT_HIDDEN, d)

    sb_k = sb_v = None
    for l in range(DEPTH):
        g_mix = mix_norm_g[l].reshape(1, d)
        if l < N_A_LAYERS:
            xt = _mixer_a(xt, g_mix, w_in_a[l].astype(BF16), conv_w[l],
                          conv_ln_g[l].reshape(1, CONV_CH), conv_ln_b[l].reshape(1, CONV_CH),
                          mem_qg[l], mk_big[l], mv_big[l], w_out_a[l].astype(BF16), tm)
        else:
            j = l - N_A_LAYERS
            q, mo = _mixer_b_pre(xt, g_mix, w_in_b[j].astype(BF16),
                                 sb_q_norm_g[j].reshape(1, SB_HEAD_DIM), mem_qg[l],
                                 mk_big[l], mv_big[l], tm)
            sb = _sb_attention(q, sb_k, sb_v)
            xt = _out_b(xt, sb, mo, w_out_b[j].astype(BF16), tm)
        xt = _moe(xt, ffn_norm_g[l].reshape(1, d), w_router[l], b_router[l],
                  moe_in[l], moe_down[l], tm)
        if l == N_A_LAYERS - 1:
            sb_k, sb_v = _kv_proj(xt, kv_norm_g.reshape(1, d), w_kv.astype(BF16),
                                  sb_k_norm_g.reshape(1, SB_HEAD_DIM), tm)
    return xt.reshape(b, t, d)
```

```python
import functools

import jax
import jax.numpy as jnp
from jax import lax
from jax.experimental import pallas as pl
from jax.experimental.pallas import tpu as pltpu

F32 = jnp.float32
BF16 = jnp.bfloat16

D_MODEL = 1024
DEPTH = 4
N_A_LAYERS = DEPTH // 2
MEM_TOKENS = 256
MEM_HEADS = 4
MEM_HEAD_DIM = 64
MEM_WIDTH = MEM_HEADS * MEM_HEAD_DIM
CONV_CH = D_MODEL - MEM_WIDTH
CONV_WIDTH = 31
SB_HEADS = 4
SB_HEAD_DIM = 128
SB_WIDTH = SB_HEADS * SB_HEAD_DIM
N_GROUPS = 4
EXPERTS_PER_GROUP = 4
N_EXPERTS = N_GROUPS * EXPERTS_PER_GROUP
EXPERT_HIDDEN = D_MODEL // 8
EPS = 1e-6

LANES = 128
SUBLANES = 8
CONV_HALO = 32
CONV_ROWS = 64
CONV_PAD = 2 * SUBLANES
SB_BLOCK = 128
SB_GROUP = 4
SB_UNROLLED_BLOCKS = 3
LOG2_E = 1.4426950408889634
SB_ZERO_LOG2 = -106.0 * LOG2_E
VMEM_LIMIT = 56 * 1024 * 1024


def _dot(a, b):
    return jnp.dot(a, b, preferred_element_type=F32)


def _split(a):
    hi = a.astype(BF16)
    lo = (a - hi.astype(F32)).astype(BF16)
    return hi, lo


def _dot_exact_rhs(a, b):
    hi, lo = _split(a)
    return _dot(hi, b) + _dot(lo, b)


def _dot3(a, b):
    ah, al = _split(a)
    bh, bl = _split(b)
    return _dot(ah, bh) + (_dot(ah, bl) + _dot(al, bh))


def _rms(x, g):
    ms = jnp.mean(x * x, axis=-1, keepdims=True)
    return x * lax.rsqrt(ms + EPS) * g


def _sigmoid(x):
    return 1.0 / (1.0 + jnp.exp(-x))


def _full_spec(shape):
    return pl.BlockSpec(shape, lambda *_: (0,) * len(shape))


def _layer_spec(stacked, layer):
    rest = stacked.shape[1:]
    return pl.BlockSpec((None,) + rest, lambda *_: (layer,) + (0,) * len(rest))


def _head_block_ones(n, head_dim):
    r = lax.broadcasted_iota(jnp.int32, (n, n), 0) // head_dim
    c = lax.broadcasted_iota(jnp.int32, (n, n), 1) // head_dim
    return (r == c).astype(BF16)


def _memkv_kernel(mem_ref, g_ref, w_ref, kg_ref, mk_ref, mv_ref):
    mem_n = _rms(mem_ref[...], g_ref[...]).astype(BF16)
    kv = _dot(mem_n, w_ref[...])
    mk = kv[:, :MEM_WIDTH]
    mv = kv[:, MEM_WIDTH:]
    ss = _dot_exact_rhs(mk * mk, _head_block_ones(MEM_WIDTH, MEM_HEAD_DIM))
    mk = mk * lax.rsqrt(ss * (1.0 / MEM_HEAD_DIM) + EPS) * (kg_ref[...] * MEM_HEAD_DIM ** -0.5)
    mk_t = mk.T
    mk_ref[...] = jnp.zeros(mk_ref.shape, mk_ref.dtype)
    lane_head = lax.broadcasted_iota(jnp.int32, mv.shape, 1) // MEM_HEAD_DIM
    for h in range(MEM_HEADS):
        rows = slice(h * MEM_HEAD_DIM, (h + 1) * MEM_HEAD_DIM)
        mk_ref[rows, h * MEM_TOKENS:(h + 1) * MEM_TOKENS] = mk_t[rows, :].astype(BF16)
        mv_ref[h * MEM_TOKENS:(h + 1) * MEM_TOKENS, :] = jnp.where(lane_head == h, mv, 0.0).astype(BF16)


def _memkv(mem, mem_norm_g, w_mem_kv, mem_k_norm_g):
    kg = jnp.tile(mem_k_norm_g, (1, MEM_HEADS)).reshape(DEPTH, 1, MEM_WIDTH)
    return pl.pallas_call(
        _memkv_kernel,
        grid=(DEPTH,),
        in_specs=[
            pl.BlockSpec((MEM_TOKENS, D_MODEL), lambda l: (0, 0)),
            pl.BlockSpec((1, D_MODEL), lambda l: (0, 0)),
            pl.BlockSpec((None, D_MODEL, 2 * MEM_WIDTH), lambda l: (l, 0, 0)),
            pl.BlockSpec((None, 1, MEM_WIDTH), lambda l: (l, 0, 0)),
        ],
        out_specs=[
            pl.BlockSpec((None, MEM_WIDTH, MEM_HEADS * MEM_TOKENS), lambda l: (l, 0, 0)),
            pl.BlockSpec((None, MEM_HEADS * MEM_TOKENS, MEM_WIDTH), lambda l: (l, 0, 0)),
        ],
        out_shape=[
            jax.ShapeDtypeStruct((DEPTH, MEM_WIDTH, MEM_HEADS * MEM_TOKENS), BF16),
            jax.ShapeDtypeStruct((DEPTH, MEM_HEADS * MEM_TOKENS, MEM_WIDTH), BF16),
        ],
        name="memkv",
    )(mem, mem_norm_g.reshape(1, D_MODEL), w_mem_kv.astype(BF16), kg)


def _mem_attention(q, qg, mk_big, mv_big):
    ss = _dot_exact_rhs(q * q, _head_block_ones(MEM_WIDTH, MEM_HEAD_DIM))
    qn = q * lax.rsqrt(ss * (1.0 / MEM_HEAD_DIM) + EPS) * qg
    s = _dot(qn.astype(BF16), mk_big)
    probs = []
    for h in range(MEM_HEADS):
        sh = s[:, h * MEM_TOKENS:(h + 1) * MEM_TOKENS]
        e = jnp.exp(sh - jnp.max(sh, axis=-1, keepdims=True))
        probs.append((e * (1.0 / jnp.sum(e, axis=-1, keepdims=True))).astype(BF16))
    return _dot(jnp.concatenate(probs, axis=1), mv_big)


def _mixer_a_kernel(x_ref, g_ref, win_ref, cw_ref, lng_ref, lnb_ref, qg_ref, mk_ref, mv_ref,
                    wout_ref, o_ref, cbuf_ref, conv_ref):
    tm = x_ref.shape[0]

    @pl.when(pl.program_id(0) == 0)
    def _():
        cbuf_ref[0:CONV_HALO, :] = jnp.zeros((CONV_HALO, CONV_CH), F32)
        cbuf_ref[CONV_HALO + tm:, :] = jnp.zeros((CONV_PAD, CONV_CH), F32)

    x = x_ref[...]
    h = _rms(x, g_ref[...]).astype(BF16)
    u = _dot(h, win_ref[...])
    cbuf_ref[CONV_HALO:CONV_HALO + tm, :] = u[:, :CONV_CH] * _sigmoid(u[:, CONV_CH:2 * CONV_CH])

    base = CONV_HALO - (CONV_WIDTH - 1)
    win = CONV_ROWS + 2 * SUBLANES

    def conv_chunk(r, carry):
        r0 = pl.multiple_of(r * CONV_ROWS, CONV_ROWS)
        for c0 in range(0, CONV_CH, LANES):
            cols = slice(c0, c0 + LANES)
            acc = jnp.zeros((CONV_ROWS, LANES), F32)
            for s in range(SUBLANES):
                part = jnp.zeros((win, LANES), F32)
                for k in range(s, CONV_WIDTH, SUBLANES):
                    start = pl.multiple_of(r0 + (k - s), SUBLANES)
                    part = part + cw_ref[k:k + 1, cols] * cbuf_ref[pl.ds(start, win), cols]
                acc = acc + part[base + s:base + s + CONV_ROWS, :]
            conv_ref[pl.ds(r0, CONV_ROWS), cols] = acc
        return carry

    lax.fori_loop(0, tm // CONV_ROWS, conv_chunk, 0)
    cbuf_ref[0:CONV_HALO, :] = cbuf_ref[tm:tm + CONV_HALO, :]

    c = conv_ref[...]
    xc = c - jnp.mean(c, axis=-1, keepdims=True)
    var = jnp.mean(xc * xc, axis=-1, keepdims=True)
    y = xc * lax.rsqrt(var + EPS) * lng_ref[...] + lnb_ref[...]
    mixed = (y * _sigmoid(y)).astype(BF16)
    mem_out = _mem_attention(u[:, 2 * CONV_CH:], qg_ref[...], mk_ref[...], mv_ref[...]).astype(BF16)
    o_ref[...] = x + _dot(mixed, wout_ref[0:CONV_CH, :]) + _dot(mem_out, wout_ref[CONV_CH:, :])


def _mixer_a(x, g, w_in, conv_w, ln_g, ln_b, qg, mk_big, mv_big, w_out, layer, tm):
    t = x.shape[0]
    full = _full_spec
    return pl.pallas_call(
        _mixer_a_kernel,
        grid=(t // tm,),
        in_specs=[
            pl.BlockSpec((tm, D_MODEL), lambda i: (i, 0)),
            full((1, D_MODEL)),
            _layer_spec(w_in, layer),
            _layer_spec(conv_w, layer),
            full((1, CONV_CH)),
            full((1, CONV_CH)),
            full((1, MEM_WIDTH)),
            _layer_spec(mk_big, layer),
            _layer_spec(mv_big, layer),
            _layer_spec(w_out, layer),
        ],
        out_specs=pl.BlockSpec((tm, D_MODEL), lambda i: (i, 0)),
        out_shape=jax.ShapeDtypeStruct((t, D_MODEL), F32),
        scratch_shapes=[
            pltpu.VMEM((tm + CONV_HALO + CONV_PAD, CONV_CH), F32),
            pltpu.VMEM((tm, CONV_CH), F32),
        ],
        compiler_params=pltpu.CompilerParams(
            dimension_semantics=("arbitrary",), vmem_limit_bytes=VMEM_LIMIT),
        name="mixer_a",
    )(x, g, w_in, conv_w, ln_g, ln_b, qg, mk_big, mv_big, w_out)


ROUTER_LANES = LANES
EXPERT_LANE0 = N_GROUPS


def _route(logits):
    neg = jnp.float32(-jnp.inf)
    lane = lax.broadcasted_iota(jnp.int32, logits.shape, 1)
    lane_f = lane.astype(F32)
    is_g = lane < N_GROUPS
    gl = jnp.where(is_g, logits, neg)
    gmax = jnp.max(gl, axis=-1, keepdims=True)
    gidx = jnp.min(jnp.where(gl == gmax, lane_f, float(ROUTER_LANES)), axis=-1, keepdims=True)
    gsum = jnp.sum(jnp.where(is_g, jnp.exp(gl - gmax), 0.0), axis=-1, keepdims=True)
    g_gate = 1.0 / gsum
    lane_group = ((lane - EXPERT_LANE0) // EXPERTS_PER_GROUP).astype(F32)
    sel = (lane >= EXPERT_LANE0) & (lane < EXPERT_LANE0 + N_EXPERTS) & (lane_group == gidx)
    sl = jnp.where(sel, logits, neg)
    m1 = jnp.max(sl, axis=-1, keepdims=True)
    i1 = jnp.min(jnp.where(sl == m1, lane_f, float(ROUTER_LANES)), axis=-1, keepdims=True)
    sl2 = jnp.where(lane_f == i1, neg, sl)
    m2 = jnp.max(sl2, axis=-1, keepdims=True)
    i2 = jnp.min(jnp.where(sl2 == m2, lane_f, float(ROUTER_LANES)), axis=-1, keepdims=True)
    e2 = jnp.exp(m2 - m1)
    w1 = g_gate / (1.0 + e2)
    w2 = w1 * e2
    return jnp.where(lane_f == i1, w1, jnp.where(lane_f == i2, w2, 0.0))


def _moe_kernel(*refs, with_out_proj):
    if with_out_proj:
        x_ref, sb_ref, mo_ref, wout_ref, g_ref, wr_ref, br_ref, win_ref, wdn_ref, o_ref, act_ref = refs
        x = (x_ref[...] + _dot(sb_ref[...], wout_ref[0:SB_WIDTH, :])
             + _dot(mo_ref[...], wout_ref[SB_WIDTH:, :]))
    else:
        x_ref, g_ref, wr_ref, br_ref, win_ref, wdn_ref, o_ref, act_ref = refs
        x = x_ref[...]
    hf = _rms(x, g_ref[...])
    hb = hf.astype(BF16)
    gate = _route(_dot3(hf, wr_ref[...]) + br_ref[...])
    for e in range(N_EXPERTS):
        hu = _dot(hb, win_ref[e])
        a = hu[:, :EXPERT_HIDDEN]
        act = a * _sigmoid(a) * hu[:, EXPERT_HIDDEN:] * gate[:, EXPERT_LANE0 + e:EXPERT_LANE0 + e + 1]
        act_ref[:, e * EXPERT_HIDDEN:(e + 1) * EXPERT_HIDDEN] = act.astype(BF16)
    o_ref[...] = x + _dot(act_ref[...], wdn_ref[...])


def _moe(x, g, w_router, b_router, w_in, w_down, layer, tm, out_proj=None):
    t = x.shape[0]
    tokens = lambda width: pl.BlockSpec((tm, width), lambda i: (i, 0))
    args, in_specs = [x], [tokens(D_MODEL)]
    if out_proj is not None:
        sb, mo, w_out, j = out_proj
        args += [sb, mo, w_out]
        in_specs += [tokens(SB_WIDTH), tokens(MEM_WIDTH), _layer_spec(w_out, j)]
    args += [g, w_router, b_router, w_in, w_down]
    in_specs += [_full_spec((1, D_MODEL)), _layer_spec(w_router, layer), _layer_spec(b_router, layer),
                 _layer_spec(w_in, layer), _layer_spec(w_down, layer)]
    return pl.pallas_call(
        functools.partial(_moe_kernel, with_out_proj=out_proj is not None),
        grid=(t // tm,),
        in_specs=in_specs,
        out_specs=tokens(D_MODEL),
        out_shape=jax.ShapeDtypeStruct((t, D_MODEL), F32),
        scratch_shapes=[pltpu.VMEM((tm, N_EXPERTS * EXPERT_HIDDEN), BF16)],
        compiler_params=pltpu.CompilerParams(
            dimension_semantics=("arbitrary",), vmem_limit_bytes=VMEM_LIMIT),
        name="moe",
    )(*args)


def _kv_kernel(x_ref, g_ref, w_ref, kg_ref, k_ref, v_ref):
    s = _rms(x_ref[...], g_ref[...]).astype(BF16)
    kv = _dot(s, w_ref[...])
    for h in range(SB_HEADS):
        cols = slice(h * SB_HEAD_DIM, (h + 1) * SB_HEAD_DIM)
        k_ref[:, cols] = _rms(kv[:, cols], kg_ref[...]).astype(BF16)
    v_ref[...] = kv[:, SB_WIDTH:].astype(BF16)


def _kv_proj(x, g, w_kv, kg, tm):
    t = x.shape[0]
    full = lambda shape: pl.BlockSpec(shape, lambda i: (0,) * len(shape))
    return pl.pallas_call(
        _kv_kernel,
        grid=(t // tm,),
        in_specs=[
            pl.BlockSpec((tm, D_MODEL), lambda i: (i, 0)),
            full((1, D_MODEL)),
            full(w_kv.shape),
            full((1, SB_HEAD_DIM)),
        ],
        out_specs=[pl.BlockSpec((tm, SB_WIDTH), lambda i: (i, 0))] * 2,
        out_shape=[jax.ShapeDtypeStruct((t, SB_WIDTH), BF16)] * 2,
        compiler_params=pltpu.CompilerParams(
            dimension_semantics=("arbitrary",), vmem_limit_bytes=VMEM_LIMIT),
        name="kv_proj",
    )(x, g, w_kv, kg)


def _mixer_b_pre_kernel(x_ref, g_ref, win_ref, sqg_ref, qg_ref, mk_ref, mv_ref, q_ref, mo_ref):
    h = _rms(x_ref[...], g_ref[...]).astype(BF16)
    u = _dot(h, win_ref[...])
    sqg = sqg_ref[...] * (SB_HEAD_DIM ** -0.5 * LOG2_E)
    for hd in range(SB_HEADS):
        cols = slice(hd * SB_HEAD_DIM, (hd + 1) * SB_HEAD_DIM)
        q_ref[:, cols] = _rms(u[:, cols], sqg).astype(BF16)
    mo_ref[...] = _mem_attention(u[:, SB_WIDTH:], qg_ref[...], mk_ref[...], mv_ref[...]).astype(BF16)


def _mixer_b_pre(x, g, w_in, sqg, qg, mk_big, mv_big, j, layer, tm):
    t = x.shape[0]
    full = _full_spec
    return pl.pallas_call(
        _mixer_b_pre_kernel,
        grid=(t // tm,),
        in_specs=[
            pl.BlockSpec((tm, D_MODEL), lambda i: (i, 0)),
            full((1, D_MODEL)),
            _layer_spec(w_in, j),
            full((1, SB_HEAD_DIM)),
            full((1, MEM_WIDTH)),
            _layer_spec(mk_big, layer),
            _layer_spec(mv_big, layer),
        ],
        out_specs=[pl.BlockSpec((tm, SB_WIDTH), lambda i: (i, 0)),
                   pl.BlockSpec((tm, MEM_WIDTH), lambda i: (i, 0))],
        out_shape=[jax.ShapeDtypeStruct((t, SB_WIDTH), BF16),
                   jax.ShapeDtypeStruct((t, MEM_WIDTH), BF16)],
        compiler_params=pltpu.CompilerParams(
            dimension_semantics=("arbitrary",), vmem_limit_bytes=VMEM_LIMIT),
        name="mixer_b_pre",
    )(x, g, w_in, sqg, qg, mk_big, mv_big)


def _sb_kernel(q_ref, k_ref, v_ref, o_ref):
    first_block = pl.program_id(1) * SB_GROUP
    row = lax.broadcasted_iota(jnp.int32, (SB_BLOCK, SB_BLOCK), 0)
    col = lax.broadcasted_iota(jnp.int32, (SB_BLOCK, SB_BLOCK), 1)
    causal = col < row
    suffix_ones = (row >= col).astype(BF16)

    def earlier_key_block(q, kb, carry, acc):
        start = pl.multiple_of(kb * SB_BLOCK, SB_BLOCK)
        k = k_ref[pl.ds(start, SB_BLOCK), :]
        v = v_ref[pl.ds(start, SB_BLOCK), :]
        z = lax.dot_general(q, k, (((1,), (1,)), ((), ())), preferred_element_type=F32)
        lf = jnp.minimum(-z, 0.0) - jnp.log(1.0 + jnp.exp2(-jnp.abs(z))) * LOG2_E
        within = _dot_exact_rhs(lf, suffix_ones)
        w = jnp.exp2(z + within + carry)
        acc = acc + _dot(w.astype(BF16), v)
        return carry + within[:, 0:1], acc

    def block_rows(g, n=1):
        return slice(g * SB_BLOCK, (g + n) * SB_BLOCK)

    def run(first_step):
        offsets = range(0 if first_step else 1 - SB_UNROLLED_BLOCKS, SB_GROUP)
        users = {off: range(max(off, 0), min(off + SB_UNROLLED_BLOCKS, SB_GROUP)) for off in offsets}
        tiles = [(g, g - off) for off in offsets for g in users[off]]

        z = {}
        for off in offsets:
            start = pl.multiple_of((first_block + off) * SB_BLOCK, SB_BLOCK)
            k = k_ref[pl.ds(start, SB_BLOCK), :]
            gs = users[off]
            zz = lax.dot_general(q_ref[block_rows(gs[0], len(gs)), :], k, (((1,), (1,)), ((), ())),
                                 preferred_element_type=F32)
            for n, g in enumerate(gs):
                z[g, g - off] = zz[block_rows(n), :]

        parts = []
        for g, back in tiles:
            zt = z[g, back]
            lf = jnp.minimum(-zt, 0.0) - jnp.log(1.0 + jnp.exp2(-jnp.abs(zt))) * LOG2_E
            if back == 0:
                lf = jnp.where(causal, lf, 0.0)
            parts.append(jnp.concatenate(_split(lf), axis=1))
        suffix2 = jnp.concatenate([suffix_ones, suffix_ones], axis=0)
        within_all = _dot(jnp.concatenate(parts, axis=0), suffix2)
        within = {t: within_all[block_rows(n), :] for n, t in enumerate(tiles)}

        carries, w = [], {}
        for g in range(SB_GROUP):
            carry = jnp.zeros((SB_BLOCK, 1), F32)
            for back in range(SB_UNROLLED_BLOCKS):
                if (g, back) not in within:
                    continue
                wt = jnp.exp2(z[g, back] + within[g, back] + carry)
                if back == 0:
                    wt = jnp.where(causal, wt, 0.0)
                w[g, back] = wt.astype(BF16)
                carry = carry + within[g, back][:, 0:1]
            carries.append(carry)

        accs = [jnp.zeros((SB_BLOCK, SB_HEAD_DIM), F32) for _ in range(SB_GROUP)]
        for off in offsets:
            start = pl.multiple_of((first_block + off) * SB_BLOCK, SB_BLOCK)
            v = v_ref[pl.ds(start, SB_BLOCK), :]
            gs = users[off]
            pv = _dot(jnp.concatenate([w[g, g - off] for g in gs], axis=0), v)
            for n, g in enumerate(gs):
                accs[g] = accs[g] + pv[block_rows(n), :]
        for g in range(SB_GROUP):
            o_ref[block_rows(g), :] = accs[g].astype(o_ref.dtype)

        worst = functools.reduce(jnp.maximum, carries)

        @pl.when(jnp.max(worst) > SB_ZERO_LOG2)
        def _():
            for g in range(SB_GROUP):
                q = q_ref[block_rows(g), :]

                def cond(state):
                    kb, carry, _ = state
                    return jnp.logical_and(kb >= 0, jnp.max(carry) > SB_ZERO_LOG2)

                def body(state):
                    kb, carry, acc = state
                    carry, acc = earlier_key_block(q, kb, carry, acc)
                    return kb - 1, carry, acc

                start = first_block + g - SB_UNROLLED_BLOCKS
                _, _, acc = lax.while_loop(cond, body, (start, carries[g], accs[g]))
                o_ref[block_rows(g), :] = acc.astype(o_ref.dtype)

    assert SB_GROUP >= SB_UNROLLED_BLOCKS - 1
    pl.when(first_block == 0)(lambda: run(True))
    pl.when(first_block > 0)(lambda: run(False))


def _sb_attention(q, k, v):
    t = q.shape[0]
    tq = SB_GROUP * SB_BLOCK
    return pl.pallas_call(
        _sb_kernel,
        grid=(SB_HEADS, t // tq),
        in_specs=[
            pl.BlockSpec((tq, SB_HEAD_DIM), lambda h, i: (i, h)),
            pl.BlockSpec((t, SB_HEAD_DIM), lambda h, i: (0, h)),
            pl.BlockSpec((t, SB_HEAD_DIM), lambda h, i: (0, h)),
        ],
        out_specs=pl.BlockSpec((tq, SB_HEAD_DIM), lambda h, i: (i, h)),
        out_shape=jax.ShapeDtypeStruct((t, SB_WIDTH), BF16),
        compiler_params=pltpu.CompilerParams(
            dimension_semantics=("arbitrary", "arbitrary"), vmem_limit_bytes=VMEM_LIMIT),
        name="sb_attention",
    )(q, k, v)


def _token_tile(t):
    tm = 512
    while t % tm:
        tm //= 2
    return tm


def kernel(x, mem, mem_norm_g, mix_norm_g, ffn_norm_g, w_in_a, conv_w, conv_ln_g, conv_ln_b, w_out_a, w_in_b, sb_q_norm_g, w_out_b, kv_norm_g, w_kv, sb_k_norm_g, w_mem_kv, mem_q_norm_g, mem_k_norm_g, router_g_w, router_g_b, router_e_w, router_e_b, moe_w_in, moe_w_down):
    b, t, d = x.shape
    assert b == 1 and d == D_MODEL and t % SB_BLOCK == 0
    tm = _token_tile(t)
    xt = x.reshape(t, d)

    mk_big, mv_big = _memkv(mem.reshape(MEM_TOKENS, d), mem_norm_g, w_mem_kv, mem_k_norm_g)
    mem_qg = jnp.tile(mem_q_norm_g, (1, MEM_HEADS)).reshape(DEPTH, 1, MEM_WIDTH)

    pad = ROUTER_LANES - N_GROUPS - N_EXPERTS
    w_router = jnp.concatenate(
        [router_g_w, router_e_w.transpose(0, 2, 1, 3).reshape(DEPTH, d, N_EXPERTS),
         jnp.zeros((DEPTH, d, pad), F32)], axis=-1)
    b_router = jnp.concatenate(
        [router_g_b, router_e_b.reshape(DEPTH, N_EXPERTS), jnp.zeros((DEPTH, pad), F32)],
        axis=-1).reshape(DEPTH, 1, ROUTER_LANES)
    moe_in = moe_w_in.astype(BF16).reshape(DEPTH, N_EXPERTS, d, 2 * EXPERT_HIDDEN)
    moe_down = moe_w_down.astype(BF16).reshape(DEPTH, N_EXPERTS * EXPERT_HIDDEN, d)

    w_in_a, w_out_a, w_in_b, w_out_b = (w.astype(BF16) for w in (w_in_a, w_out_a, w_in_b, w_out_b))

    sb_k = sb_v = None
    for l in range(DEPTH):
        g_mix = mix_norm_g[l].reshape(1, d)
        out_proj = None
        if l < N_A_LAYERS:
            xt = _mixer_a(xt, g_mix, w_in_a, conv_w,
                          conv_ln_g[l].reshape(1, CONV_CH), conv_ln_b[l].reshape(1, CONV_CH),
                          mem_qg[l], mk_big, mv_big, w_out_a, l, tm)
        else:
            j = l - N_A_LAYERS
            q, mo = _mixer_b_pre(xt, g_mix, w_in_b, sb_q_norm_g[j].reshape(1, SB_HEAD_DIM),
                                 mem_qg[l], mk_big, mv_big, j, l, tm)
            out_proj = (_sb_attention(q, sb_k, sb_v), mo, w_out_b, j)
        xt = _moe(xt, ffn_norm_g[l].reshape(1, d), w_router, b_router, moe_in, moe_down, l, tm,
                  out_proj=out_proj)
        if l == N_A_LAYERS - 1:
            sb_k, sb_v = _kv_proj(xt, kv_norm_g.reshape(1, d), w_kv.astype(BF16),
                                  sb_k_norm_g.reshape(1, SB_HEAD_DIM), tm)
    return xt.reshape(b, t, d)
```

```python
import functools

import jax
import jax.numpy as jnp
from jax import lax
from jax.experimental import pallas as pl
from jax.experimental.pallas import tpu as pltpu

F32 = jnp.float32
BF16 = jnp.bfloat16

D_MODEL = 1024
DEPTH = 4
N_A_LAYERS = DEPTH // 2
MEM_TOKENS = 256
MEM_HEADS = 4
MEM_HEAD_DIM = 64
MEM_WIDTH = MEM_HEADS * MEM_HEAD_DIM
CONV_CH = D_MODEL - MEM_WIDTH
CONV_WIDTH = 31
SB_HEADS = 4
SB_HEAD_DIM = 128
SB_WIDTH = SB_HEADS * SB_HEAD_DIM
N_GROUPS = 4
EXPERTS_PER_GROUP = 4
N_EXPERTS = N_GROUPS * EXPERTS_PER_GROUP
EXPERT_HIDDEN = D_MODEL // 8
EPS = 1e-6

LANES = 128
SUBLANES = 8
CONV_HALO = 32
CONV_ROWS = 128
CONV_PAD = 2 * SUBLANES
SB_BLOCK = 128
SB_GROUP = 8
SB_UNROLLED_BLOCKS = 3
LOG2_E = 1.4426950408889634
SB_ZERO_LOG2 = -106.0 * LOG2_E
VMEM_LIMIT = 56 * 1024 * 1024
TOKEN_TILE = 512
MOE_TOKEN_TILE = 1024


def _dot(a, b):
    return jnp.dot(a, b, preferred_element_type=F32)


def _split(a):
    hi = a.astype(BF16)
    lo = (a - hi.astype(F32)).astype(BF16)
    return hi, lo


def _dot_exact_rhs(a, b):
    hi, lo = _split(a)
    return _dot(hi, b) + _dot(lo, b)


def _dot3(a, b):
    ah, al = _split(a)
    bh, bl = _split(b)
    n = b.shape[1]
    both = _dot(ah, jnp.concatenate([bh, bl], axis=1))
    return both[:, :n] + (both[:, n:] + _dot(al, bh))


def _rms(x, g):
    ms = jnp.mean(x * x, axis=-1, keepdims=True)
    return x * lax.rsqrt(ms + EPS) * g


def _sigmoid(x):
    return 1.0 / (1.0 + jnp.exp(-x))


def _full_spec(shape):
    return pl.BlockSpec(shape, lambda *_: (0,) * len(shape), pipeline_mode=pl.Buffered(1))


def _layer_spec(stacked, layer):
    rest = stacked.shape[1:]
    return pl.BlockSpec((None,) + rest, lambda *_: (layer,) + (0,) * len(rest),
                        pipeline_mode=pl.Buffered(1))


def _head_block_ones(n, head_dim):
    r = lax.broadcasted_iota(jnp.int32, (n, n), 0) // head_dim
    c = lax.broadcasted_iota(jnp.int32, (n, n), 1) // head_dim
    return (r == c).astype(BF16)


def _memkv_kernel(mem_ref, g_ref, w_ref, kg_ref, mk_ref, mv_ref):
    mem_n = _rms(mem_ref[...], g_ref[...]).astype(BF16)
    kv = _dot(mem_n, w_ref[...])
    mk = kv[:, :MEM_WIDTH]
    mv = kv[:, MEM_WIDTH:]
    ss = _dot_exact_rhs(mk * mk, _head_block_ones(MEM_WIDTH, MEM_HEAD_DIM))
    mk = mk * lax.rsqrt(ss * (1.0 / MEM_HEAD_DIM) + EPS) * (kg_ref[...] * MEM_HEAD_DIM ** -0.5)
    mk_t = mk.T
    mk_ref[...] = jnp.zeros(mk_ref.shape, mk_ref.dtype)
    lane_head = lax.broadcasted_iota(jnp.int32, mv.shape, 1) // MEM_HEAD_DIM
    for h in range(MEM_HEADS):
        rows = slice(h * MEM_HEAD_DIM, (h + 1) * MEM_HEAD_DIM)
        mk_ref[rows, h * MEM_TOKENS:(h + 1) * MEM_TOKENS] = mk_t[rows, :].astype(BF16)
        mv_ref[h * MEM_TOKENS:(h + 1) * MEM_TOKENS, :] = jnp.where(lane_head == h, mv, 0.0).astype(BF16)


def _memkv(mem, mem_norm_g, w_mem_kv, mem_k_norm_g):
    kg = jnp.tile(mem_k_norm_g, (1, MEM_HEADS)).reshape(DEPTH, 1, MEM_WIDTH)
    return pl.pallas_call(
        _memkv_kernel,
        grid=(DEPTH,),
        in_specs=[
            pl.BlockSpec((MEM_TOKENS, D_MODEL), lambda l: (0, 0)),
            pl.BlockSpec((1, D_MODEL), lambda l: (0, 0)),
            pl.BlockSpec((None, D_MODEL, 2 * MEM_WIDTH), lambda l: (l, 0, 0)),
            pl.BlockSpec((None, 1, MEM_WIDTH), lambda l: (l, 0, 0)),
        ],
        out_specs=[
            pl.BlockSpec((None, MEM_WIDTH, MEM_HEADS * MEM_TOKENS), lambda l: (l, 0, 0)),
            pl.BlockSpec((None, MEM_HEADS * MEM_TOKENS, MEM_WIDTH), lambda l: (l, 0, 0)),
        ],
        out_shape=[
            jax.ShapeDtypeStruct((DEPTH, MEM_WIDTH, MEM_HEADS * MEM_TOKENS), BF16),
            jax.ShapeDtypeStruct((DEPTH, MEM_HEADS * MEM_TOKENS, MEM_WIDTH), BF16),
        ],
        name="memkv",
    )(mem, mem_norm_g.reshape(1, D_MODEL), w_mem_kv.astype(BF16), kg)


def _mem_attention(q, qg, mk_big, mv_big):
    ss = _dot_exact_rhs(q * q, _head_block_ones(MEM_WIDTH, MEM_HEAD_DIM))
    qn = q * lax.rsqrt(ss * (1.0 / MEM_HEAD_DIM) + EPS) * qg
    s = _dot(qn.astype(BF16), mk_big)
    probs = []
    for h in range(MEM_HEADS):
        sh = s[:, h * MEM_TOKENS:(h + 1) * MEM_TOKENS]
        e = jnp.exp(sh - jnp.max(sh, axis=-1, keepdims=True))
        probs.append((e * (1.0 / jnp.sum(e, axis=-1, keepdims=True))).astype(BF16))
    return _dot(jnp.concatenate(probs, axis=1), mv_big)


def _mixer_a_kernel(x_ref, g_ref, win_ref, cw_ref, lng_ref, lnb_ref, qg_ref, mk_ref, mv_ref,
                    wout_ref, o_ref, cbuf_ref, conv_ref):
    tm = x_ref.shape[0]

    @pl.when(pl.program_id(0) == 0)
    def _():
        cbuf_ref[0:CONV_HALO, :] = jnp.zeros((CONV_HALO, CONV_CH), F32)
        cbuf_ref[CONV_HALO + tm:, :] = jnp.zeros((CONV_PAD, CONV_CH), F32)

    x = x_ref[...]
    h = _rms(x, g_ref[...]).astype(BF16)
    u = _dot(h, win_ref[...])
    cbuf_ref[CONV_HALO:CONV_HALO + tm, :] = u[:, :CONV_CH] * _sigmoid(u[:, CONV_CH:2 * CONV_CH])

    base = CONV_HALO - (CONV_WIDTH - 1)
    win = CONV_ROWS + SUBLANES

    def conv_chunk(r, carry):
        r0 = pl.multiple_of(r * CONV_ROWS, CONV_ROWS)
        for c0 in range(0, CONV_CH, LANES):
            cols = slice(c0, c0 + LANES)
            acc = None
            for shift in range(SUBLANES):
                part = None
                for k in range(CONV_WIDTH):
                    if (base + k) % SUBLANES != shift:
                        continue
                    start = pl.multiple_of(r0 + (base + k - shift), SUBLANES)
                    term = cw_ref[k:k + 1, cols] * cbuf_ref[pl.ds(start, win), cols]
                    part = term if part is None else part + term
                part = part[shift:shift + CONV_ROWS, :]
                acc = part if acc is None else acc + part
            conv_ref[pl.ds(r0, CONV_ROWS), cols] = acc
        return carry

    lax.fori_loop(0, tm // CONV_ROWS, conv_chunk, 0)
    cbuf_ref[0:CONV_HALO, :] = cbuf_ref[tm:tm + CONV_HALO, :]

    c = conv_ref[...]
    xc = c - jnp.mean(c, axis=-1, keepdims=True)
    var = jnp.mean(xc * xc, axis=-1, keepdims=True)
    y = xc * lax.rsqrt(var + EPS) * lng_ref[...] + lnb_ref[...]
    mixed = (y * _sigmoid(y)).astype(BF16)
    mem_out = _mem_attention(u[:, 2 * CONV_CH:], qg_ref[...], mk_ref[...], mv_ref[...]).astype(BF16)
    o_ref[...] = x + _dot(mixed, wout_ref[0:CONV_CH, :]) + _dot(mem_out, wout_ref[CONV_CH:, :])


def _mixer_a(x, g, w_in, conv_w, ln_g, ln_b, qg, mk_big, mv_big, w_out, layer, tm):
    t = x.shape[0]
    full = _full_spec
    return pl.pallas_call(
        _mixer_a_kernel,
        grid=(t // tm,),
        in_specs=[
            pl.BlockSpec((tm, D_MODEL), lambda i: (i, 0)),
            full((1, D_MODEL)),
            _layer_spec(w_in, layer),
            _layer_spec(conv_w, layer),
            full((1, CONV_CH)),
            full((1, CONV_CH)),
            full((1, MEM_WIDTH)),
            _layer_spec(mk_big, layer),
            _layer_spec(mv_big, layer),
            _layer_spec(w_out, layer),
        ],
        out_specs=pl.BlockSpec((tm, D_MODEL), lambda i: (i, 0)),
        out_shape=jax.ShapeDtypeStruct((t, D_MODEL), F32),
        scratch_shapes=[
            pltpu.VMEM((tm + CONV_HALO + CONV_PAD, CONV_CH), F32),
            pltpu.VMEM((tm, CONV_CH), F32),
        ],
        compiler_params=pltpu.CompilerParams(
            dimension_semantics=("arbitrary",), vmem_limit_bytes=VMEM_LIMIT),
        name="mixer_a",
    )(x, g, w_in, conv_w, ln_g, ln_b, qg, mk_big, mv_big, w_out)


ROUTER_LANES = LANES
EXPERT_LANE0 = N_GROUPS


def _route(logits):
    neg = jnp.float32(-jnp.inf)
    lane = lax.broadcasted_iota(jnp.int32, logits.shape, 1)
    lane_f = lane.astype(F32)
    is_g = lane < N_GROUPS
    gl = jnp.where(is_g, logits, neg)
    gmax = jnp.max(gl, axis=-1, keepdims=True)
    gidx = jnp.min(jnp.where(gl == gmax, lane_f, float(ROUTER_LANES)), axis=-1, keepdims=True)
    gsum = jnp.sum(jnp.where(is_g, jnp.exp(gl - gmax), 0.0), axis=-1, keepdims=True)
    g_gate = 1.0 / gsum
    lane_group = ((lane - EXPERT_LANE0) // EXPERTS_PER_GROUP).astype(F32)
    sel = (lane >= EXPERT_LANE0) & (lane < EXPERT_LANE0 + N_EXPERTS) & (lane_group == gidx)
    sl = jnp.where(sel, logits, neg)
    m1 = jnp.max(sl, axis=-1, keepdims=True)
    i1 = jnp.min(jnp.where(sl == m1, lane_f, float(ROUTER_LANES)), axis=-1, keepdims=True)
    sl2 = jnp.where(lane_f == i1, neg, sl)
    m2 = jnp.max(sl2, axis=-1, keepdims=True)
    i2 = jnp.min(jnp.where(sl2 == m2, lane_f, float(ROUTER_LANES)), axis=-1, keepdims=True)
    e2 = jnp.exp(m2 - m1)
    w1 = g_gate / (1.0 + e2)
    w2 = w1 * e2
    return jnp.where(lane_f == i1, w1, jnp.where(lane_f == i2, w2, 0.0))


def _moe_kernel(*refs, with_out_proj):
    if with_out_proj:
        x_ref, sb_ref, mo_ref, wout_ref, g_ref, wr_ref, br_ref, win_ref, wdn_ref, o_ref, act_ref = refs
        x = (x_ref[...] + _dot(sb_ref[...], wout_ref[0:SB_WIDTH, :])
             + _dot(mo_ref[...], wout_ref[SB_WIDTH:, :]))
    else:
        x_ref, g_ref, wr_ref, br_ref, win_ref, wdn_ref, o_ref, act_ref = refs
        x = x_ref[...]
    hf = _rms(x, g_ref[...])
    hb = hf.astype(BF16)
    gate = _route(_dot3(hf, wr_ref[...]) + br_ref[...])
    for e in range(N_EXPERTS):
        hu = _dot(hb, win_ref[e])
        a = hu[:, :EXPERT_HIDDEN]
        act = a * _sigmoid(a) * hu[:, EXPERT_HIDDEN:] * gate[:, EXPERT_LANE0 + e:EXPERT_LANE0 + e + 1]
        act_ref[:, e * EXPERT_HIDDEN:(e + 1) * EXPERT_HIDDEN] = act.astype(BF16)
    o_ref[...] = x + _dot(act_ref[...], wdn_ref[...])


def _moe(x, g, w_router, b_router, w_in, w_down, layer, tm, out_proj=None):
    t = x.shape[0]
    tokens = lambda width: pl.BlockSpec((tm, width), lambda i: (i, 0))
    args, in_specs = [x], [tokens(D_MODEL)]
    if out_proj is not None:
        sb, mo, w_out, j = out_proj
        args += [sb, mo, w_out]
        in_specs += [tokens(SB_WIDTH), tokens(MEM_WIDTH), _layer_spec(w_out, j)]
    args += [g, w_router, b_router, w_in, w_down]
    in_specs += [_full_spec((1, D_MODEL)), _layer_spec(w_router, layer), _layer_spec(b_router, layer),
                 _layer_spec(w_in, layer), _layer_spec(w_down, layer)]
    return pl.pallas_call(
        functools.partial(_moe_kernel, with_out_proj=out_proj is not None),
        grid=(t // tm,),
        in_specs=in_specs,
        out_specs=tokens(D_MODEL),
        out_shape=jax.ShapeDtypeStruct((t, D_MODEL), F32),
        scratch_shapes=[pltpu.VMEM((tm, N_EXPERTS * EXPERT_HIDDEN), BF16)],
        compiler_params=pltpu.CompilerParams(
            dimension_semantics=("arbitrary",), vmem_limit_bytes=VMEM_LIMIT),
        name="moe",
    )(*args)


def _kv_kernel(x_ref, g_ref, w_ref, kg_ref, k_ref, v_ref):
    s = _rms(x_ref[...], g_ref[...]).astype(BF16)
    kv = _dot(s, w_ref[...])
    for h in range(SB_HEADS):
        cols = slice(h * SB_HEAD_DIM, (h + 1) * SB_HEAD_DIM)
        k_ref[:, cols] = _rms(kv[:, cols], kg_ref[...]).astype(BF16)
    v_ref[...] = kv[:, SB_WIDTH:].astype(BF16)


def _kv_proj(x, g, w_kv, kg, tm):
    t = x.shape[0]
    full = lambda shape: pl.BlockSpec(shape, lambda i: (0,) * len(shape))
    return pl.pallas_call(
        _kv_kernel,
        grid=(t // tm,),
        in_specs=[
            pl.BlockSpec((tm, D_MODEL), lambda i: (i, 0)),
            full((1, D_MODEL)),
            full(w_kv.shape),
            full((1, SB_HEAD_DIM)),
        ],
        out_specs=[pl.BlockSpec((tm, SB_WIDTH), lambda i: (i, 0))] * 2,
        out_shape=[jax.ShapeDtypeStruct((t, SB_WIDTH), BF16)] * 2,
        compiler_params=pltpu.CompilerParams(
            dimension_semantics=("arbitrary",), vmem_limit_bytes=VMEM_LIMIT),
        name="kv_proj",
    )(x, g, w_kv, kg)


def _mixer_b_pre_kernel(x_ref, g_ref, win_ref, sqg_ref, qg_ref, mk_ref, mv_ref, q_ref, mo_ref):
    h = _rms(x_ref[...], g_ref[...]).astype(BF16)
    u = _dot(h, win_ref[...])
    sqg = sqg_ref[...] * (SB_HEAD_DIM ** -0.5 * LOG2_E)
    for hd in range(SB_HEADS):
        cols = slice(hd * SB_HEAD_DIM, (hd + 1) * SB_HEAD_DIM)
        q_ref[:, cols] = _rms(u[:, cols], sqg).astype(BF16)
    mo_ref[...] = _mem_attention(u[:, SB_WIDTH:], qg_ref[...], mk_ref[...], mv_ref[...]).astype(BF16)


def _mixer_b_pre(x, g, w_in, sqg, qg, mk_big, mv_big, j, layer, tm):
    t = x.shape[0]
    full = _full_spec
    return pl.pallas_call(
        _mixer_b_pre_kernel,
        grid=(t // tm,),
        in_specs=[
            pl.BlockSpec((tm, D_MODEL), lambda i: (i, 0)),
            full((1, D_MODEL)),
            _layer_spec(w_in, j),
            full((1, SB_HEAD_DIM)),
            full((1, MEM_WIDTH)),
            _layer_spec(mk_big, layer),
            _layer_spec(mv_big, layer),
        ],
        out_specs=[pl.BlockSpec((tm, SB_WIDTH), lambda i: (i, 0)),
                   pl.BlockSpec((tm, MEM_WIDTH), lambda i: (i, 0))],
        out_shape=[jax.ShapeDtypeStruct((t, SB_WIDTH), BF16),
                   jax.ShapeDtypeStruct((t, MEM_WIDTH), BF16)],
        compiler_params=pltpu.CompilerParams(
            dimension_semantics=("arbitrary",), vmem_limit_bytes=VMEM_LIMIT),
        name="mixer_b_pre",
    )(x, g, w_in, sqg, qg, mk_big, mv_big)


def _sb_kernel(q_ref, k_ref, v_ref, o_ref):
    first_block = pl.program_id(1) * SB_GROUP
    row = lax.broadcasted_iota(jnp.int32, (SB_BLOCK, SB_BLOCK), 0)
    col = lax.broadcasted_iota(jnp.int32, (SB_BLOCK, SB_BLOCK), 1)
    causal = col < row
    suffix_ones = (row >= col).astype(BF16)

    def earlier_key_block(q, kb, carry, acc):
        start = pl.multiple_of(kb * SB_BLOCK, SB_BLOCK)
        k = k_ref[pl.ds(start, SB_BLOCK), :]
        v = v_ref[pl.ds(start, SB_BLOCK), :]
        z = lax.dot_general(q, k, (((1,), (1,)), ((), ())), preferred_element_type=F32)
        lf = jnp.minimum(-z, 0.0) - jnp.log(1.0 + jnp.exp2(-jnp.abs(z))) * LOG2_E
        within = _dot_exact_rhs(lf, suffix_ones)
        w = jnp.exp2(z + within + carry)
        acc = acc + _dot(w.astype(BF16), v)
        return carry + within[:, 0:1], acc

    def block_rows(g, n=1):
        return slice(g * SB_BLOCK, (g + n) * SB_BLOCK)

    def run(first_step):
        offsets = range(0 if first_step else 1 - SB_UNROLLED_BLOCKS, SB_GROUP)
        users = {off: range(max(off, 0), min(off + SB_UNROLLED_BLOCKS, SB_GROUP)) for off in offsets}
        tiles = [(g, g - off) for off in offsets for g in users[off]]

        z = {}
        for off in offsets:
            start = pl.multiple_of((first_block + off) * SB_BLOCK, SB_BLOCK)
            k = k_ref[pl.ds(start, SB_BLOCK), :]
            gs = users[off]
            zz = lax.dot_general(q_ref[block_rows(gs[0], len(gs)), :], k, (((1,), (1,)), ((), ())),
                                 preferred_element_type=F32)
            for n, g in enumerate(gs):
                z[g, g - off] = zz[block_rows(n), :]

        parts = []
        for g, back in tiles:
            zt = z[g, back]
            lf = jnp.minimum(-zt, 0.0) - jnp.log(1.0 + jnp.exp2(-jnp.abs(zt))) * LOG2_E
            if back == 0:
                lf = jnp.where(causal, lf, 0.0)
            parts.append(jnp.concatenate(_split(lf), axis=1))
        suffix2 = jnp.concatenate([suffix_ones, suffix_ones], axis=0)
        within_all = _dot(jnp.concatenate(parts, axis=0), suffix2)
        within = {t: within_all[block_rows(n), :] for n, t in enumerate(tiles)}

        carries, w = [], {}
        for g in range(SB_GROUP):
            carry = jnp.zeros((SB_BLOCK, 1), F32)
            for back in range(SB_UNROLLED_BLOCKS):
                if (g, back) not in within:
                    continue
                wt = jnp.exp2(z[g, back] + within[g, back] + carry)
                if back == 0:
                    wt = jnp.where(causal, wt, 0.0)
                w[g, back] = wt.astype(BF16)
                carry = carry + within[g, back][:, 0:1]
            carries.append(carry)

        accs = [jnp.zeros((SB_BLOCK, SB_HEAD_DIM), F32) for _ in range(SB_GROUP)]
        for off in offsets:
            start = pl.multiple_of((first_block + off) * SB_BLOCK, SB_BLOCK)
            v = v_ref[pl.ds(start, SB_BLOCK), :]
            gs = users[off]
            pv = _dot(jnp.concatenate([w[g, g - off] for g in gs], axis=0), v)
            for n, g in enumerate(gs):
                accs[g] = accs[g] + pv[block_rows(n), :]
        for g in range(SB_GROUP):
            o_ref[block_rows(g), :] = accs[g].astype(o_ref.dtype)

        worst = functools.reduce(jnp.maximum, carries)

        @pl.when(jnp.max(worst) > SB_ZERO_LOG2)
        def _():
            for g in range(SB_GROUP):
                q = q_ref[block_rows(g), :]

                def cond(state):
                    kb, carry, _ = state
                    return jnp.logical_and(kb >= 0, jnp.max(carry) > SB_ZERO_LOG2)

                def body(state):
                    kb, carry, acc = state
                    carry, acc = earlier_key_block(q, kb, carry, acc)
                    return kb - 1, carry, acc

                start = first_block + g - SB_UNROLLED_BLOCKS
                _, _, acc = lax.while_loop(cond, body, (start, carries[g], accs[g]))
                o_ref[block_rows(g), :] = acc.astype(o_ref.dtype)

    assert SB_GROUP >= SB_UNROLLED_BLOCKS - 1
    pl.when(first_block == 0)(lambda: run(True))
    pl.when(first_block > 0)(lambda: run(False))


def _sb_attention(q, k, v):
    t = q.shape[0]
    tq = SB_GROUP * SB_BLOCK
    return pl.pallas_call(
        _sb_kernel,
        grid=(SB_HEADS, t // tq),
        in_specs=[
            pl.BlockSpec((tq, SB_HEAD_DIM), lambda h, i: (i, h)),
            pl.BlockSpec((t, SB_HEAD_DIM), lambda h, i: (0, h)),
            pl.BlockSpec((t, SB_HEAD_DIM), lambda h, i: (0, h)),
        ],
        out_specs=pl.BlockSpec((tq, SB_HEAD_DIM), lambda h, i: (i, h)),
        out_shape=jax.ShapeDtypeStruct((t, SB_WIDTH), BF16),
        compiler_params=pltpu.CompilerParams(
            dimension_semantics=("arbitrary", "arbitrary"), vmem_limit_bytes=VMEM_LIMIT),
        name="sb_attention",
    )(q, k, v)


def _token_tile(t, tm):
    while t % tm:
        tm //= 2
    return tm


def kernel(x, mem, mem_norm_g, mix_norm_g, ffn_norm_g, w_in_a, conv_w, conv_ln_g, conv_ln_b, w_out_a, w_in_b, sb_q_norm_g, w_out_b, kv_norm_g, w_kv, sb_k_norm_g, w_mem_kv, mem_q_norm_g, mem_k_norm_g, router_g_w, router_g_b, router_e_w, router_e_b, moe_w_in, moe_w_down):
    b, t, d = x.shape
    assert b == 1 and d == D_MODEL and t % SB_BLOCK == 0
    tm = _token_tile(t, TOKEN_TILE)
    tm_moe = _token_tile(t, MOE_TOKEN_TILE)
    xt = x.reshape(t, d)

    mk_big, mv_big = _memkv(mem.reshape(MEM_TOKENS, d), mem_norm_g, w_mem_kv, mem_k_norm_g)
    mem_qg = jnp.tile(mem_q_norm_g, (1, MEM_HEADS)).reshape(DEPTH, 1, MEM_WIDTH)

    pad = ROUTER_LANES - N_GROUPS - N_EXPERTS
    w_router = jnp.concatenate(
        [router_g_w, router_e_w.transpose(0, 2, 1, 3).reshape(DEPTH, d, N_EXPERTS),
         jnp.zeros((DEPTH, d, pad), F32)], axis=-1)
    b_router = jnp.concatenate(
        [router_g_b, router_e_b.reshape(DEPTH, N_EXPERTS), jnp.zeros((DEPTH, pad), F32)],
        axis=-1).reshape(DEPTH, 1, ROUTER_LANES)
    moe_in = moe_w_in.astype(BF16).reshape(DEPTH, N_EXPERTS, d, 2 * EXPERT_HIDDEN)
    moe_down = moe_w_down.astype(BF16).reshape(DEPTH, N_EXPERTS * EXPERT_HIDDEN, d)

    w_in_a, w_out_a, w_in_b, w_out_b = (w.astype(BF16) for w in (w_in_a, w_out_a, w_in_b, w_out_b))

    sb_k = sb_v = None
    for l in range(DEPTH):
        g_mix = mix_norm_g[l].reshape(1, d)
        out_proj = None
        if l < N_A_LAYERS:
            xt = _mixer_a(xt, g_mix, w_in_a, conv_w,
                          conv_ln_g[l].reshape(1, CONV_CH), conv_ln_b[l].reshape(1, CONV_CH),
                          mem_qg[l], mk_big, mv_big, w_out_a, l, tm)
        else:
            j = l - N_A_LAYERS
            q, mo = _mixer_b_pre(xt, g_mix, w_in_b, sb_q_norm_g[j].reshape(1, SB_HEAD_DIM),
                                 mem_qg[l], mk_big, mv_big, j, l, tm)
            out_proj = (_sb_attention(q, sb_k, sb_v), mo, w_out_b, j)
        xt = _moe(xt, ffn_norm_g[l].reshape(1, d), w_router, b_router, moe_in, moe_down, l, tm_moe,
                  out_proj=out_proj)
        if l == N_A_LAYERS - 1:
            sb_k, sb_v = _kv_proj(xt, kv_norm_g.reshape(1, d), w_kv.astype(BF16),
                                  sb_k_norm_g.reshape(1, SB_HEAD_DIM), tm)
    return xt.reshape(b, t, d)
```

```python
import functools

import jax
import jax.numpy as jnp
from jax import lax
from jax.experimental import pallas as pl
from jax.experimental.pallas import tpu as pltpu

F32 = jnp.float32
BF16 = jnp.bfloat16

D_MODEL = 1024
DEPTH = 4
N_A_LAYERS = DEPTH // 2
MEM_TOKENS = 256
MEM_HEADS = 4
MEM_HEAD_DIM = 64
MEM_WIDTH = MEM_HEADS * MEM_HEAD_DIM
CONV_CH = D_MODEL - MEM_WIDTH
CONV_WIDTH = 31
SB_HEADS = 4
SB_HEAD_DIM = 128
SB_WIDTH = SB_HEADS * SB_HEAD_DIM
N_GROUPS = 4
EXPERTS_PER_GROUP = 4
N_EXPERTS = N_GROUPS * EXPERTS_PER_GROUP
EXPERT_HIDDEN = D_MODEL // 8
EPS = 1e-6

LANES = 128
SUBLANES = 8
CONV_HALO = 32
CONV_ROWS = 128
CONV_PAD = 2 * SUBLANES
SB_BLOCK = 128
SB_GROUP = 8
SB_UNROLLED_BLOCKS = 3
LOG2_E = 1.4426950408889634
SB_ZERO_LOG2 = -106.0 * LOG2_E
VMEM_LIMIT = 56 * 1024 * 1024
TOKEN_TILE = 512
MOE_TOKEN_TILE = 1024


def _dot(a, b):
    return jnp.dot(a, b, preferred_element_type=F32)


def _split(a):
    hi = a.astype(BF16)
    lo = (a - hi.astype(F32)).astype(BF16)
    return hi, lo


def _dot_exact_rhs(a, b):
    hi, lo = _split(a)
    return _dot(hi, b) + _dot(lo, b)


def _dot3(a, b):
    ah, al = _split(a)
    bh, bl = _split(b)
    n = b.shape[1]
    both = _dot(ah, jnp.concatenate([bh, bl], axis=1))
    return both[:, :n] + (both[:, n:] + _dot(al, bh))


def _rms(x, g):
    ms = jnp.mean(x * x, axis=-1, keepdims=True)
    return x * lax.rsqrt(ms + EPS) * g


def _sigmoid(x):
    return 1.0 / (1.0 + jnp.exp(-x))


def _full_spec(shape):
    return pl.BlockSpec(shape, lambda *_: (0,) * len(shape), pipeline_mode=pl.Buffered(1))


def _layer_spec(stacked, layer):
    rest = stacked.shape[1:]
    return pl.BlockSpec((None,) + rest, lambda *_: (layer,) + (0,) * len(rest),
                        pipeline_mode=pl.Buffered(1))


def _head_block_ones(n, head_dim):
    r = lax.broadcasted_iota(jnp.int32, (n, n), 0) // head_dim
    c = lax.broadcasted_iota(jnp.int32, (n, n), 1) // head_dim
    return (r == c).astype(BF16)


def _memkv_kernel(mem_ref, g_ref, w_ref, kg_ref, mk_ref, mv_ref):
    mem_n = _rms(mem_ref[...], g_ref[...]).astype(BF16)
    kv = _dot(mem_n, w_ref[...])
    mk = kv[:, :MEM_WIDTH]
    mv = kv[:, MEM_WIDTH:]
    ss = _dot_exact_rhs(mk * mk, _head_block_ones(MEM_WIDTH, MEM_HEAD_DIM))
    mk = mk * lax.rsqrt(ss * (1.0 / MEM_HEAD_DIM) + EPS) * (kg_ref[...] * MEM_HEAD_DIM ** -0.5)
    mk_t = mk.T
    mk_ref[...] = jnp.zeros(mk_ref.shape, mk_ref.dtype)
    lane_head = lax.broadcasted_iota(jnp.int32, mv.shape, 1) // MEM_HEAD_DIM
    for h in range(MEM_HEADS):
        rows = slice(h * MEM_HEAD_DIM, (h + 1) * MEM_HEAD_DIM)
        mk_ref[rows, h * MEM_TOKENS:(h + 1) * MEM_TOKENS] = mk_t[rows, :].astype(BF16)
        mv_ref[h * MEM_TOKENS:(h + 1) * MEM_TOKENS, :] = jnp.where(lane_head == h, mv, 0.0).astype(BF16)


def _memkv(mem, mem_norm_g, w_mem_kv, mem_k_norm_g):
    kg = jnp.tile(mem_k_norm_g, (1, MEM_HEADS)).reshape(DEPTH, 1, MEM_WIDTH)
    return pl.pallas_call(
        _memkv_kernel,
        grid=(DEPTH,),
        in_specs=[
            pl.BlockSpec((MEM_TOKENS, D_MODEL), lambda l: (0, 0)),
            pl.BlockSpec((1, D_MODEL), lambda l: (0, 0)),
            pl.BlockSpec((None, D_MODEL, 2 * MEM_WIDTH), lambda l: (l, 0, 0)),
            pl.BlockSpec((None, 1, MEM_WIDTH), lambda l: (l, 0, 0)),
        ],
        out_specs=[
            pl.BlockSpec((None, MEM_WIDTH, MEM_HEADS * MEM_TOKENS), lambda l: (l, 0, 0)),
            pl.BlockSpec((None, MEM_HEADS * MEM_TOKENS, MEM_WIDTH), lambda l: (l, 0, 0)),
        ],
        out_shape=[
            jax.ShapeDtypeStruct((DEPTH, MEM_WIDTH, MEM_HEADS * MEM_TOKENS), BF16),
            jax.ShapeDtypeStruct((DEPTH, MEM_HEADS * MEM_TOKENS, MEM_WIDTH), BF16),
        ],
        name="memkv",
    )(mem, mem_norm_g.reshape(1, D_MODEL), w_mem_kv.astype(BF16), kg)


def _mem_attention(q, qg, mk_big, mv_big):
    ss = _dot_exact_rhs(q * q, _head_block_ones(MEM_WIDTH, MEM_HEAD_DIM))
    qn = q * lax.rsqrt(ss * (1.0 / MEM_HEAD_DIM) + EPS) * qg
    s = _dot(qn.astype(BF16), mk_big)
    probs = []
    for h in range(MEM_HEADS):
        sh = s[:, h * MEM_TOKENS:(h + 1) * MEM_TOKENS]
        e = jnp.exp(sh - jnp.max(sh, axis=-1, keepdims=True))
        probs.append((e * (1.0 / jnp.sum(e, axis=-1, keepdims=True))).astype(BF16))
    return _dot(jnp.concatenate(probs, axis=1), mv_big)


def _mixer_a_kernel(x_ref, g_ref, win_ref, cw_ref, lng_ref, lnb_ref, qg_ref, mk_ref, mv_ref,
                    wout_ref, o_ref, cbuf_ref, conv_ref, xprev_ref, moprev_ref):
    tm = x_ref.shape[0]

    @pl.when(pl.program_id(0) == 0)
    def _():
        for ref in (cbuf_ref, xprev_ref, moprev_ref):
            ref[...] = jnp.zeros(ref.shape, ref.dtype)

    x = x_ref[...]
    h = _rms(x, g_ref[...]).astype(BF16)
    u = _dot(h, win_ref[...])

    base = CONV_HALO - (CONV_WIDTH - 1)
    win = CONV_ROWS + SUBLANES
    for r0 in range(0, tm, CONV_ROWS):
        for c0 in range(0, CONV_CH, LANES):
            cols = slice(c0, c0 + LANES)
            acc = None
            for shift in range(SUBLANES):
                part = None
                for k in range(CONV_WIDTH):
                    if (base + k) % SUBLANES != shift:
                        continue
                    start = r0 + (base + k - shift)
                    term = cw_ref[k:k + 1, cols] * cbuf_ref[start:start + win, cols]
                    part = term if part is None else part + term
                part = part[shift:shift + CONV_ROWS, :]
                acc = part if acc is None else acc + part
            conv_ref[r0:r0 + CONV_ROWS, cols] = acc

    cbuf_ref[0:CONV_HALO, :] = cbuf_ref[tm:tm + CONV_HALO, :]
    cbuf_ref[CONV_HALO:CONV_HALO + tm, :] = u[:, :CONV_CH] * _sigmoid(u[:, CONV_CH:2 * CONV_CH])

    c = conv_ref[...]
    xc = c - jnp.mean(c, axis=-1, keepdims=True)
    var = jnp.mean(xc * xc, axis=-1, keepdims=True)
    y = xc * lax.rsqrt(var + EPS) * lng_ref[...] + lnb_ref[...]
    mixed = (y * _sigmoid(y)).astype(BF16)
    o_ref[...] = (xprev_ref[...] + _dot(mixed, wout_ref[0:CONV_CH, :])
                  + _dot(moprev_ref[...], wout_ref[CONV_CH:, :]))

    xprev_ref[...] = x
    moprev_ref[...] = _mem_attention(u[:, 2 * CONV_CH:], qg_ref[...], mk_ref[...],
                                     mv_ref[...]).astype(BF16)


def _mixer_a(x, g, w_in, conv_w, ln_g, ln_b, qg, mk_big, mv_big, w_out, layer, tm):
    t = x.shape[0]
    n = t // tm
    full = _full_spec
    return pl.pallas_call(
        _mixer_a_kernel,
        grid=(n + 1,),
        in_specs=[
            pl.BlockSpec((tm, D_MODEL), lambda i: (jnp.minimum(i, n - 1), 0)),
            full((1, D_MODEL)),
            _layer_spec(w_in, layer),
            _layer_spec(conv_w, layer),
            full((1, CONV_CH)),
            full((1, CONV_CH)),
            full((1, MEM_WIDTH)),
            _layer_spec(mk_big, layer),
            _layer_spec(mv_big, layer),
            _layer_spec(w_out, layer),
        ],
        out_specs=pl.BlockSpec((tm, D_MODEL), lambda i: (jnp.maximum(i - 1, 0), 0)),
        out_shape=jax.ShapeDtypeStruct((t, D_MODEL), F32),
        scratch_shapes=[
            pltpu.VMEM((tm + CONV_HALO + CONV_PAD, CONV_CH), F32),
            pltpu.VMEM((tm, CONV_CH), F32),
            pltpu.VMEM((tm, D_MODEL), F32),
            pltpu.VMEM((tm, MEM_WIDTH), BF16),
        ],
        compiler_params=pltpu.CompilerParams(
            dimension_semantics=("arbitrary",), vmem_limit_bytes=VMEM_LIMIT),
        name="mixer_a",
    )(x, g, w_in, conv_w, ln_g, ln_b, qg, mk_big, mv_big, w_out)


ROUTER_LANES = LANES
EXPERT_LANE0 = N_GROUPS


def _route(logits):
    neg = jnp.float32(-jnp.inf)
    lane = lax.broadcasted_iota(jnp.int32, logits.shape, 1)
    lane_f = lane.astype(F32)
    is_g = lane < N_GROUPS
    gl = jnp.where(is_g, logits, neg)
    gmax = jnp.max(gl, axis=-1, keepdims=True)
    gidx = jnp.min(jnp.where(gl == gmax, lane_f, float(ROUTER_LANES)), axis=-1, keepdims=True)
    gsum = jnp.sum(jnp.where(is_g, jnp.exp(gl - gmax), 0.0), axis=-1, keepdims=True)
    g_gate = 1.0 / gsum
    lane_group = ((lane - EXPERT_LANE0) // EXPERTS_PER_GROUP).astype(F32)
    sel = (lane >= EXPERT_LANE0) & (lane < EXPERT_LANE0 + N_EXPERTS) & (lane_group == gidx)
    sl = jnp.where(sel, logits, neg)
    m1 = jnp.max(sl, axis=-1, keepdims=True)
    i1 = jnp.min(jnp.where(sl == m1, lane_f, float(ROUTER_LANES)), axis=-1, keepdims=True)
    sl2 = jnp.where(lane_f == i1, neg, sl)
    m2 = jnp.max(sl2, axis=-1, keepdims=True)
    i2 = jnp.min(jnp.where(sl2 == m2, lane_f, float(ROUTER_LANES)), axis=-1, keepdims=True)
    e2 = jnp.exp(m2 - m1)
    w1 = g_gate / (1.0 + e2)
    w2 = w1 * e2
    return jnp.where(lane_f == i1, w1, jnp.where(lane_f == i2, w2, 0.0))


def _moe_kernel(*refs, with_out_proj):
    if with_out_proj:
        x_ref, sb_ref, mo_ref, wout_ref, g_ref, wr_ref, br_ref, win_ref, wdn_ref, o_ref, act_ref = refs
        x = (x_ref[...] + _dot(sb_ref[...], wout_ref[0:SB_WIDTH, :])
             + _dot(mo_ref[...], wout_ref[SB_WIDTH:, :]))
    else:
        x_ref, g_ref, wr_ref, br_ref, win_ref, wdn_ref, o_ref, act_ref = refs
        x = x_ref[...]
    hf = _rms(x, g_ref[...])
    hb = hf.astype(BF16)
    gate = _route(_dot3(hf, wr_ref[...]) + br_ref[...])
    for e in range(N_EXPERTS):
        hu = _dot(hb, win_ref[e])
        a = hu[:, :EXPERT_HIDDEN]
        act = a * _sigmoid(a) * hu[:, EXPERT_HIDDEN:] * gate[:, EXPERT_LANE0 + e:EXPERT_LANE0 + e + 1]
        act_ref[:, e * EXPERT_HIDDEN:(e + 1) * EXPERT_HIDDEN] = act.astype(BF16)
    o_ref[...] = x + _dot(act_ref[...], wdn_ref[...])


def _moe(x, g, w_router, b_router, w_in, w_down, layer, tm, out_proj=None):
    t = x.shape[0]
    tokens = lambda width: pl.BlockSpec((tm, width), lambda i: (i, 0))
    args, in_specs = [x], [tokens(D_MODEL)]
    if out_proj is not None:
        sb, mo, w_out, j = out_proj
        args += [sb, mo, w_out]
        in_specs += [tokens(SB_WIDTH), tokens(MEM_WIDTH), _layer_spec(w_out, j)]
    args += [g, w_router, b_router, w_in, w_down]
    in_specs += [_full_spec((1, D_MODEL)), _layer_spec(w_router, layer), _layer_spec(b_router, layer),
                 _layer_spec(w_in, layer), _layer_spec(w_down, layer)]
    return pl.pallas_call(
        functools.partial(_moe_kernel, with_out_proj=out_proj is not None),
        grid=(t // tm,),
        in_specs=in_specs,
        out_specs=tokens(D_MODEL),
        out_shape=jax.ShapeDtypeStruct((t, D_MODEL), F32),
        scratch_shapes=[pltpu.VMEM((tm, N_EXPERTS * EXPERT_HIDDEN), BF16)],
        compiler_params=pltpu.CompilerParams(
            dimension_semantics=("arbitrary",), vmem_limit_bytes=VMEM_LIMIT),
        name="moe",
    )(*args)


def _kv_kernel(x_ref, g_ref, w_ref, kg_ref, k_ref, v_ref):
    s = _rms(x_ref[...], g_ref[...]).astype(BF16)
    kv = _dot(s, w_ref[...])
    for h in range(SB_HEADS):
        cols = slice(h * SB_HEAD_DIM, (h + 1) * SB_HEAD_DIM)
        k_ref[:, cols] = _rms(kv[:, cols], kg_ref[...]).astype(BF16)
    v_ref[...] = kv[:, SB_WIDTH:].astype(BF16)


def _kv_proj(x, g, w_kv, kg, tm):
    t = x.shape[0]
    full = lambda shape: pl.BlockSpec(shape, lambda i: (0,) * len(shape))
    return pl.pallas_call(
        _kv_kernel,
        grid=(t // tm,),
        in_specs=[
            pl.BlockSpec((tm, D_MODEL), lambda i: (i, 0)),
            full((1, D_MODEL)),
            full(w_kv.shape),
            full((1, SB_HEAD_DIM)),
        ],
        out_specs=[pl.BlockSpec((tm, SB_WIDTH), lambda i: (i, 0))] * 2,
        out_shape=[jax.ShapeDtypeStruct((t, SB_WIDTH), BF16)] * 2,
        compiler_params=pltpu.CompilerParams(
            dimension_semantics=("arbitrary",), vmem_limit_bytes=VMEM_LIMIT),
        name="kv_proj",
    )(x, g, w_kv, kg)


def _mixer_b_pre_kernel(x_ref, g_ref, win_ref, sqg_ref, qg_ref, mk_ref, mv_ref, q_ref, mo_ref):
    h = _rms(x_ref[...], g_ref[...]).astype(BF16)
    u = _dot(h, win_ref[...])
    sqg = sqg_ref[...] * (SB_HEAD_DIM ** -0.5 * LOG2_E)
    for hd in range(SB_HEADS):
        cols = slice(hd * SB_HEAD_DIM, (hd + 1) * SB_HEAD_DIM)
        q_ref[:, cols] = _rms(u[:, cols], sqg).astype(BF16)
    mo_ref[...] = _mem_attention(u[:, SB_WIDTH:], qg_ref[...], mk_ref[...], mv_ref[...]).astype(BF16)


def _mixer_b_pre(x, g, w_in, sqg, qg, mk_big, mv_big, j, layer, tm):
    t = x.shape[0]
    full = _full_spec
    return pl.pallas_call(
        _mixer_b_pre_kernel,
        grid=(t // tm,),
        in_specs=[
            pl.BlockSpec((tm, D_MODEL), lambda i: (i, 0)),
            full((1, D_MODEL)),
            _layer_spec(w_in, j),
            full((1, SB_HEAD_DIM)),
            full((1, MEM_WIDTH)),
            _layer_spec(mk_big, layer),
            _layer_spec(mv_big, layer),
        ],
        out_specs=[pl.BlockSpec((tm, SB_WIDTH), lambda i: (i, 0)),
                   pl.BlockSpec((tm, MEM_WIDTH), lambda i: (i, 0))],
        out_shape=[jax.ShapeDtypeStruct((t, SB_WIDTH), BF16),
                   jax.ShapeDtypeStruct((t, MEM_WIDTH), BF16)],
        compiler_params=pltpu.CompilerParams(
            dimension_semantics=("arbitrary",), vmem_limit_bytes=VMEM_LIMIT),
        name="mixer_b_pre",
    )(x, g, w_in, sqg, qg, mk_big, mv_big)


def _sb_kernel(q_ref, k_ref, v_ref, o_ref):
    first_block = pl.program_id(1) * SB_GROUP
    row = lax.broadcasted_iota(jnp.int32, (SB_BLOCK, SB_BLOCK), 0)
    col = lax.broadcasted_iota(jnp.int32, (SB_BLOCK, SB_BLOCK), 1)
    causal = col < row
    suffix_ones = (row >= col).astype(BF16)

    def earlier_key_block(q, kb, carry, acc):
        start = pl.multiple_of(kb * SB_BLOCK, SB_BLOCK)
        k = k_ref[pl.ds(start, SB_BLOCK), :]
        v = v_ref[pl.ds(start, SB_BLOCK), :]
        z = lax.dot_general(q, k, (((1,), (1,)), ((), ())), preferred_element_type=F32)
        lf = jnp.minimum(-z, 0.0) - jnp.log(1.0 + jnp.exp2(-jnp.abs(z))) * LOG2_E
        within = _dot_exact_rhs(lf, suffix_ones)
        w = jnp.exp2(z + within + carry)
        acc = acc + _dot(w.astype(BF16), v)
        return carry + within[:, 0:1], acc

    def block_rows(g, n=1):
        return slice(g * SB_BLOCK, (g + n) * SB_BLOCK)

    def run(first_step):
        offsets = range(0 if first_step else 1 - SB_UNROLLED_BLOCKS, SB_GROUP)
        users = {off: range(max(off, 0), min(off + SB_UNROLLED_BLOCKS, SB_GROUP)) for off in offsets}
        tiles = [(g, g - off) for off in offsets for g in users[off]]

        z = {}
        for off in offsets:
            start = pl.multiple_of((first_block + off) * SB_BLOCK, SB_BLOCK)
            k = k_ref[pl.ds(start, SB_BLOCK), :]
            gs = users[off]
            zz = lax.dot_general(q_ref[block_rows(gs[0], len(gs)), :], k, (((1,), (1,)), ((), ())),
                                 preferred_element_type=F32)
            for n, g in enumerate(gs):
                z[g, g - off] = zz[block_rows(n), :]

        parts = []
        for g, back in tiles:
            zt = z[g, back]
            lf = jnp.minimum(-zt, 0.0) - jnp.log(1.0 + jnp.exp2(-jnp.abs(zt))) * LOG2_E
            if back == 0:
                lf = jnp.where(causal, lf, 0.0)
            parts.append(jnp.concatenate(_split(lf), axis=1))
        suffix2 = jnp.concatenate([suffix_ones, suffix_ones], axis=0)
        within_all = _dot(jnp.concatenate(parts, axis=0), suffix2)
        within = {t: within_all[block_rows(n), :] for n, t in enumerate(tiles)}

        carries, w = [], {}
        for g in range(SB_GROUP):
            carry = jnp.zeros((SB_BLOCK, 1), F32)
            for back in range(SB_UNROLLED_BLOCKS):
                if (g, back) not in within:
                    continue
                wt = jnp.exp2(z[g, back] + within[g, back] + carry)
                if back == 0:
                    wt = jnp.where(causal, wt, 0.0)
                w[g, back] = wt.astype(BF16)
                carry = carry + within[g, back][:, 0:1]
            carries.append(carry)

        accs = [jnp.zeros((SB_BLOCK, SB_HEAD_DIM), F32) for _ in range(SB_GROUP)]
        for off in offsets:
            start = pl.multiple_of((first_block + off) * SB_BLOCK, SB_BLOCK)
            v = v_ref[pl.ds(start, SB_BLOCK), :]
            gs = users[off]
            pv = _dot(jnp.concatenate([w[g, g - off] for g in gs], axis=0), v)
            for n, g in enumerate(gs):
                accs[g] = accs[g] + pv[block_rows(n), :]
        for g in range(SB_GROUP):
            o_ref[block_rows(g), :] = accs[g].astype(o_ref.dtype)

        worst = functools.reduce(jnp.maximum, carries)

        @pl.when(jnp.max(worst) > SB_ZERO_LOG2)
        def _():
            for g in range(SB_GROUP):
                q = q_ref[block_rows(g), :]

                def cond(state):
                    kb, carry, _ = state
                    return jnp.logical_and(kb >= 0, jnp.max(carry) > SB_ZERO_LOG2)

                def body(state):
                    kb, carry, acc = state
                    carry, acc = earlier_key_block(q, kb, carry, acc)
                    return kb - 1, carry, acc

                start = first_block + g - SB_UNROLLED_BLOCKS
                _, _, acc = lax.while_loop(cond, body, (start, carries[g], accs[g]))
                o_ref[block_rows(g), :] = acc.astype(o_ref.dtype)

    assert SB_GROUP >= SB_UNROLLED_BLOCKS - 1
    pl.when(first_block == 0)(lambda: run(True))
    pl.when(first_block > 0)(lambda: run(False))


def _sb_attention(q, k, v):
    t = q.shape[0]
    tq = SB_GROUP * SB_BLOCK
    return pl.pallas_call(
        _sb_kernel,
        grid=(SB_HEADS, t // tq),
        in_specs=[
            pl.BlockSpec((tq, SB_HEAD_DIM), lambda h, i: (i, h)),
            pl.BlockSpec((t, SB_HEAD_DIM), lambda h, i: (0, h)),
            pl.BlockSpec((t, SB_HEAD_DIM), lambda h, i: (0, h)),
        ],
        out_specs=pl.BlockSpec((tq, SB_HEAD_DIM), lambda h, i: (i, h)),
        out_shape=jax.ShapeDtypeStruct((t, SB_WIDTH), BF16),
        compiler_params=pltpu.CompilerParams(
            dimension_semantics=("arbitrary", "arbitrary"), vmem_limit_bytes=VMEM_LIMIT),
        name="sb_attention",
    )(q, k, v)


def _token_tile(t, tm):
    while t % tm:
        tm //= 2
    return tm


def kernel(x, mem, mem_norm_g, mix_norm_g, ffn_norm_g, w_in_a, conv_w, conv_ln_g, conv_ln_b, w_out_a, w_in_b, sb_q_norm_g, w_out_b, kv_norm_g, w_kv, sb_k_norm_g, w_mem_kv, mem_q_norm_g, mem_k_norm_g, router_g_w, router_g_b, router_e_w, router_e_b, moe_w_in, moe_w_down):
    b, t, d = x.shape
    assert b == 1 and d == D_MODEL and t % SB_BLOCK == 0
    tm = _token_tile(t, TOKEN_TILE)
    tm_moe = _token_tile(t, MOE_TOKEN_TILE)
    xt = x.reshape(t, d)

    mk_big, mv_big = _memkv(mem.reshape(MEM_TOKENS, d), mem_norm_g, w_mem_kv, mem_k_norm_g)
    mem_qg = jnp.tile(mem_q_norm_g, (1, MEM_HEADS)).reshape(DEPTH, 1, MEM_WIDTH)

    pad = ROUTER_LANES - N_GROUPS - N_EXPERTS
    w_router = jnp.concatenate(
        [router_g_w, router_e_w.transpose(0, 2, 1, 3).reshape(DEPTH, d, N_EXPERTS),
         jnp.zeros((DEPTH, d, pad), F32)], axis=-1)
    b_router = jnp.concatenate(
        [router_g_b, router_e_b.reshape(DEPTH, N_EXPERTS), jnp.zeros((DEPTH, pad), F32)],
        axis=-1).reshape(DEPTH, 1, ROUTER_LANES)
    moe_in = moe_w_in.astype(BF16).reshape(DEPTH, N_EXPERTS, d, 2 * EXPERT_HIDDEN)
    moe_down = moe_w_down.astype(BF16).reshape(DEPTH, N_EXPERTS * EXPERT_HIDDEN, d)

    w_in_a, w_out_a, w_in_b, w_out_b = (w.astype(BF16) for w in (w_in_a, w_out_a, w_in_b, w_out_b))

    sb_k = sb_v = None
    for l in range(DEPTH):
        g_mix = mix_norm_g[l].reshape(1, d)
        out_proj = None
        if l < N_A_LAYERS:
            xt = _mixer_a(xt, g_mix, w_in_a, conv_w,
                          conv_ln_g[l].reshape(1, CONV_CH), conv_ln_b[l].reshape(1, CONV_CH),
                          mem_qg[l], mk_big, mv_big, w_out_a, l, tm)
        else:
            j = l - N_A_LAYERS
            q, mo = _mixer_b_pre(xt, g_mix, w_in_b, sb_q_norm_g[j].reshape(1, SB_HEAD_DIM),
                                 mem_qg[l], mk_big, mv_big, j, l, tm)
            out_proj = (_sb_attention(q, sb_k, sb_v), mo, w_out_b, j)
        xt = _moe(xt, ffn_norm_g[l].reshape(1, d), w_router, b_router, moe_in, moe_down, l, tm_moe,
                  out_proj=out_proj)
        if l == N_A_LAYERS - 1:
            sb_k, sb_v = _kv_proj(xt, kv_norm_g.reshape(1, d), w_kv.astype(BF16),
                                  sb_k_norm_g.reshape(1, SB_HEAD_DIM), tm)
    return xt.reshape(b, t, d)
```

```python
import functools

import jax
import jax.numpy as jnp
from jax import lax
from jax.experimental import pallas as pl
from jax.experimental.pallas import tpu as pltpu

F32 = jnp.float32
BF16 = jnp.bfloat16

D_MODEL = 1024
DEPTH = 4
N_A_LAYERS = DEPTH // 2
MEM_TOKENS = 256
MEM_HEADS = 4
MEM_HEAD_DIM = 64
MEM_WIDTH = MEM_HEADS * MEM_HEAD_DIM
CONV_CH = D_MODEL - MEM_WIDTH
CONV_WIDTH = 31
SB_HEADS = 4
SB_HEAD_DIM = 128
SB_WIDTH = SB_HEADS * SB_HEAD_DIM
N_GROUPS = 4
EXPERTS_PER_GROUP = 4
N_EXPERTS = N_GROUPS * EXPERTS_PER_GROUP
EXPERT_HIDDEN = D_MODEL // 8
EPS = 1e-6

LANES = 128
SUBLANES = 8
CONV_HALO = 32
CONV_ROWS = 128
CONV_PAD = 2 * SUBLANES
SB_BLOCK = 128
SB_GROUP = 8
SB_UNROLLED_BLOCKS = 3
LOG2_E = 1.4426950408889634
SB_ZERO_LOG2 = -106.0 * LOG2_E
VMEM_LIMIT = 56 * 1024 * 1024
TOKEN_TILE = 512
MOE_TOKEN_TILE = 1024


def _dot(a, b):
    return jnp.dot(a, b, preferred_element_type=F32)


def _split(a):
    hi = a.astype(BF16)
    lo = (a - hi.astype(F32)).astype(BF16)
    return hi, lo


def _dot_exact_rhs(a, b):
    hi, lo = _split(a)
    return _dot(hi, b) + _dot(lo, b)


def _dot3(a, b):
    ah, al = _split(a)
    bh, bl = _split(b)
    n = b.shape[1]
    both = _dot(ah, jnp.concatenate([bh, bl], axis=1))
    return both[:, :n] + (both[:, n:] + _dot(al, bh))


def _rms(x, g):
    ms = jnp.mean(x * x, axis=-1, keepdims=True)
    return x * lax.rsqrt(ms + EPS) * g


def _sigmoid(x):
    return 1.0 / (1.0 + jnp.exp(-x))


def _full_spec(shape):
    return pl.BlockSpec(shape, lambda *_: (0,) * len(shape), pipeline_mode=pl.Buffered(1))


def _layer_spec(stacked, layer):
    rest = stacked.shape[1:]
    return pl.BlockSpec((None,) + rest, lambda *_: (layer,) + (0,) * len(rest),
                        pipeline_mode=pl.Buffered(1))


def _head_block_ones(n, head_dim):
    r = lax.broadcasted_iota(jnp.int32, (n, n), 0) // head_dim
    c = lax.broadcasted_iota(jnp.int32, (n, n), 1) // head_dim
    return (r == c).astype(BF16)


def _memkv_kernel(mem_ref, g_ref, w_ref, kg_ref, mk_ref, mv_ref):
    mem_n = _rms(mem_ref[...], g_ref[...]).astype(BF16)
    kv = _dot(mem_n, w_ref[...])
    mk = kv[:, :MEM_WIDTH]
    mv = kv[:, MEM_WIDTH:]
    ss = _dot_exact_rhs(mk * mk, _head_block_ones(MEM_WIDTH, MEM_HEAD_DIM))
    mk = mk * lax.rsqrt(ss * (1.0 / MEM_HEAD_DIM) + EPS) * (kg_ref[...] * MEM_HEAD_DIM ** -0.5)
    mk_t = mk.T
    mk_ref[...] = jnp.zeros(mk_ref.shape, mk_ref.dtype)
    lane_head = lax.broadcasted_iota(jnp.int32, mv.shape, 1) // MEM_HEAD_DIM
    for h in range(MEM_HEADS):
        rows = slice(h * MEM_HEAD_DIM, (h + 1) * MEM_HEAD_DIM)
        mk_ref[rows, h * MEM_TOKENS:(h + 1) * MEM_TOKENS] = mk_t[rows, :].astype(BF16)
        mv_ref[h * MEM_TOKENS:(h + 1) * MEM_TOKENS, :] = jnp.where(lane_head == h, mv, 0.0).astype(BF16)


def _memkv(mem, mem_norm_g, w_mem_kv, mem_k_norm_g):
    kg = jnp.tile(mem_k_norm_g, (1, MEM_HEADS)).reshape(DEPTH, 1, MEM_WIDTH)
    return pl.pallas_call(
        _memkv_kernel,
        grid=(DEPTH,),
        in_specs=[
            pl.BlockSpec((MEM_TOKENS, D_MODEL), lambda l: (0, 0)),
            pl.BlockSpec((1, D_MODEL), lambda l: (0, 0)),
            pl.BlockSpec((None, D_MODEL, 2 * MEM_WIDTH), lambda l: (l, 0, 0)),
            pl.BlockSpec((None, 1, MEM_WIDTH), lambda l: (l, 0, 0)),
        ],
        out_specs=[
            pl.BlockSpec((None, MEM_WIDTH, MEM_HEADS * MEM_TOKENS), lambda l: (l, 0, 0)),
            pl.BlockSpec((None, MEM_HEADS * MEM_TOKENS, MEM_WIDTH), lambda l: (l, 0, 0)),
        ],
        out_shape=[
            jax.ShapeDtypeStruct((DEPTH, MEM_WIDTH, MEM_HEADS * MEM_TOKENS), BF16),
            jax.ShapeDtypeStruct((DEPTH, MEM_HEADS * MEM_TOKENS, MEM_WIDTH), BF16),
        ],
        name="memkv",
    )(mem, mem_norm_g.reshape(1, D_MODEL), w_mem_kv.astype(BF16), kg)


def _mem_attention(q, qg, mk_big, mv_big):
    ss = _dot_exact_rhs(q * q, _head_block_ones(MEM_WIDTH, MEM_HEAD_DIM))
    qn = q * lax.rsqrt(ss * (1.0 / MEM_HEAD_DIM) + EPS) * qg
    s = _dot(qn.astype(BF16), mk_big)
    probs = []
    for h in range(MEM_HEADS):
        sh = s[:, h * MEM_TOKENS:(h + 1) * MEM_TOKENS]
        e = jnp.exp(sh - jnp.max(sh, axis=-1, keepdims=True))
        probs.append((e * (1.0 / jnp.sum(e, axis=-1, keepdims=True))).astype(BF16))
    return _dot(jnp.concatenate(probs, axis=1), mv_big)


def _mixer_a_kernel(x_ref, g_ref, win_ref, cw_ref, lng_ref, lnb_ref, qg_ref, mk_ref, mv_ref,
                    wout_ref, o_ref, cbuf_ref, conv_ref, xprev_ref, moprev_ref):
    tm = x_ref.shape[0]

    @pl.when(pl.program_id(0) == 0)
    def _():
        for ref in (cbuf_ref, xprev_ref, moprev_ref):
            ref[...] = jnp.zeros(ref.shape, ref.dtype)

    x = x_ref[...]
    h = _rms(x, g_ref[...]).astype(BF16)
    u = _dot(h, win_ref[...])

    base = CONV_HALO - (CONV_WIDTH - 1)
    win = CONV_ROWS + SUBLANES
    for r0 in range(0, tm, CONV_ROWS):
        for c0 in range(0, CONV_CH, LANES):
            cols = slice(c0, c0 + LANES)
            acc = None
            for shift in range(SUBLANES):
                part = None
                for k in range(CONV_WIDTH):
                    if (base + k) % SUBLANES != shift:
                        continue
                    start = r0 + (base + k - shift)
                    term = cw_ref[k:k + 1, cols] * cbuf_ref[start:start + win, cols]
                    part = term if part is None else part + term
                part = part[shift:shift + CONV_ROWS, :]
                acc = part if acc is None else acc + part
            conv_ref[r0:r0 + CONV_ROWS, cols] = acc

    cbuf_ref[0:CONV_HALO, :] = cbuf_ref[tm:tm + CONV_HALO, :]
    cbuf_ref[CONV_HALO:CONV_HALO + tm, :] = u[:, :CONV_CH] * _sigmoid(u[:, CONV_CH:2 * CONV_CH])

    c = conv_ref[...]
    xc = c - jnp.mean(c, axis=-1, keepdims=True)
    var = jnp.mean(xc * xc, axis=-1, keepdims=True)
    y = xc * lax.rsqrt(var + EPS) * lng_ref[...] + lnb_ref[...]
    mixed = (y * _sigmoid(y)).astype(BF16)
    o_ref[...] = (xprev_ref[...] + _dot(mixed, wout_ref[0:CONV_CH, :])
                  + _dot(moprev_ref[...], wout_ref[CONV_CH:, :]))

    xprev_ref[...] = x
    moprev_ref[...] = _mem_attention(u[:, 2 * CONV_CH:], qg_ref[...], mk_ref[...],
                                     mv_ref[...]).astype(BF16)


def _mixer_a(x, g, w_in, conv_w, ln_g, ln_b, qg, mk_big, mv_big, w_out, layer, tm):
    t = x.shape[0]
    n = t // tm
    full = _full_spec
    return pl.pallas_call(
        _mixer_a_kernel,
        grid=(n + 1,),
        in_specs=[
            pl.BlockSpec((tm, D_MODEL), lambda i: (jnp.minimum(i, n - 1), 0)),
            full((1, D_MODEL)),
            _layer_spec(w_in, layer),
            _layer_spec(conv_w, layer),
            full((1, CONV_CH)),
            full((1, CONV_CH)),
            full((1, MEM_WIDTH)),
            _layer_spec(mk_big, layer),
            _layer_spec(mv_big, layer),
            _layer_spec(w_out, layer),
        ],
        out_specs=pl.BlockSpec((tm, D_MODEL), lambda i: (jnp.maximum(i - 1, 0), 0)),
        out_shape=jax.ShapeDtypeStruct((t, D_MODEL), F32),
        scratch_shapes=[
            pltpu.VMEM((tm + CONV_HALO + CONV_PAD, CONV_CH), F32),
            pltpu.VMEM((tm, CONV_CH), F32),
            pltpu.VMEM((tm, D_MODEL), F32),
            pltpu.VMEM((tm, MEM_WIDTH), BF16),
        ],
        compiler_params=pltpu.CompilerParams(
            dimension_semantics=("arbitrary",), vmem_limit_bytes=VMEM_LIMIT),
        name="mixer_a",
    )(x, g, w_in, conv_w, ln_g, ln_b, qg, mk_big, mv_big, w_out)


ROUTER_LANES = LANES
EXPERT_LANE0 = N_GROUPS


def _route(logits):
    neg = jnp.float32(-jnp.inf)
    lane = lax.broadcasted_iota(jnp.int32, logits.shape, 1)
    lane_f = lane.astype(F32)
    is_g = lane < N_GROUPS
    gl = jnp.where(is_g, logits, neg)
    gmax = jnp.max(gl, axis=-1, keepdims=True)
    gidx = jnp.min(jnp.where(gl == gmax, lane_f, float(ROUTER_LANES)), axis=-1, keepdims=True)
    gsum = jnp.sum(jnp.where(is_g, jnp.exp(gl - gmax), 0.0), axis=-1, keepdims=True)
    g_gate = 1.0 / gsum
    lane_group = ((lane - EXPERT_LANE0) // EXPERTS_PER_GROUP).astype(F32)
    sel = (lane >= EXPERT_LANE0) & (lane < EXPERT_LANE0 + N_EXPERTS) & (lane_group == gidx)
    sl = jnp.where(sel, logits, neg)
    m1 = jnp.max(sl, axis=-1, keepdims=True)
    i1 = jnp.min(jnp.where(sl == m1, lane_f, float(ROUTER_LANES)), axis=-1, keepdims=True)
    sl2 = jnp.where(lane_f == i1, neg, sl)
    m2 = jnp.max(sl2, axis=-1, keepdims=True)
    i2 = jnp.min(jnp.where(sl2 == m2, lane_f, float(ROUTER_LANES)), axis=-1, keepdims=True)
    e2 = jnp.exp(m2 - m1)
    w1 = g_gate / (1.0 + e2)
    w2 = w1 * e2
    return jnp.where(lane_f == i1, w1, jnp.where(lane_f == i2, w2, 0.0))


def _moe_kernel(*refs, with_out_proj):
    if with_out_proj:
        x_ref, sb_ref, mo_ref, wout_ref, g_ref, wr_ref, br_ref, win_ref, wdn_ref, o_ref, act_ref = refs
        x = (x_ref[...] + _dot(sb_ref[...], wout_ref[0:SB_WIDTH, :])
             + _dot(mo_ref[...], wout_ref[SB_WIDTH:, :]))
    else:
        x_ref, g_ref, wr_ref, br_ref, win_ref, wdn_ref, o_ref, act_ref = refs
        x = x_ref[...]
    hf = _rms(x, g_ref[...])
    hb = hf.astype(BF16)
    gate = _route(_dot3(hf, wr_ref[...]) + br_ref[...])
    for e in range(N_EXPERTS):
        hu = _dot(hb, win_ref[e])
        a = hu[:, :EXPERT_HIDDEN]
        act = a * _sigmoid(a) * hu[:, EXPERT_HIDDEN:] * gate[:, EXPERT_LANE0 + e:EXPERT_LANE0 + e + 1]
        act_ref[:, e * EXPERT_HIDDEN:(e + 1) * EXPERT_HIDDEN] = act.astype(BF16)
    o_ref[...] = x + _dot(act_ref[...], wdn_ref[...])


def _moe(x, g, w_router, b_router, w_in, w_down, layer, tm, out_proj=None):
    t = x.shape[0]
    tokens = lambda width: pl.BlockSpec((tm, width), lambda i: (i, 0))
    args, in_specs = [x], [tokens(D_MODEL)]
    if out_proj is not None:
        sb, mo, w_out, j = out_proj
        args += [sb, mo, w_out]
        in_specs += [tokens(SB_WIDTH), tokens(MEM_WIDTH), _layer_spec(w_out, j)]
    args += [g, w_router, b_router, w_in, w_down]
    in_specs += [_full_spec((1, D_MODEL)), _layer_spec(w_router, layer), _layer_spec(b_router, layer),
                 _layer_spec(w_in, layer), _layer_spec(w_down, layer)]
    return pl.pallas_call(
        functools.partial(_moe_kernel, with_out_proj=out_proj is not None),
        grid=(t // tm,),
        in_specs=in_specs,
        out_specs=tokens(D_MODEL),
        out_shape=jax.ShapeDtypeStruct((t, D_MODEL), F32),
        scratch_shapes=[pltpu.VMEM((tm, N_EXPERTS * EXPERT_HIDDEN), BF16)],
        compiler_params=pltpu.CompilerParams(
            dimension_semantics=("arbitrary",), vmem_limit_bytes=VMEM_LIMIT),
        name="moe",
    )(*args)


def _sb_queries(u, sqg_ref, q_ref):
    sqg = sqg_ref[...] * (SB_HEAD_DIM ** -0.5 * LOG2_E)
    for hd in range(SB_HEADS):
        cols = slice(hd * SB_HEAD_DIM, (hd + 1) * SB_HEAD_DIM)
        q_ref[:, cols] = _rms(u[:, cols], sqg).astype(BF16)


def _mixer_b_pre_kernel(*refs, with_kv):
    if with_kv:
        (x_ref, g_ref, win_ref, sqg_ref, qg_ref, mk_ref, mv_ref, kvg_ref, wkv_ref, kg_ref,
         q_ref, mo_ref, k_ref, v_ref) = refs
    else:
        x_ref, g_ref, win_ref, sqg_ref, qg_ref, mk_ref, mv_ref, q_ref, mo_ref = refs
    x = x_ref[...]
    unit = x * lax.rsqrt(jnp.mean(x * x, axis=-1, keepdims=True) + EPS)
    u = _dot((unit * g_ref[...]).astype(BF16), win_ref[...])
    _sb_queries(u, sqg_ref, q_ref)
    mo_ref[...] = _mem_attention(u[:, SB_WIDTH:], qg_ref[...], mk_ref[...], mv_ref[...]).astype(BF16)
    if with_kv:
        kv = _dot((unit * kvg_ref[...]).astype(BF16), wkv_ref[...])
        for h in range(SB_HEADS):
            cols = slice(h * SB_HEAD_DIM, (h + 1) * SB_HEAD_DIM)
            k_ref[:, cols] = _rms(kv[:, cols], kg_ref[...]).astype(BF16)
        v_ref[...] = kv[:, SB_WIDTH:].astype(BF16)


def _mixer_b_pre(x, g, w_in, sqg, qg, mk_big, mv_big, j, layer, tm, kv=None):
    t = x.shape[0]
    full = _full_spec
    tokens = lambda width: pl.BlockSpec((tm, width), lambda i: (i, 0))
    args = [x, g, w_in, sqg, qg, mk_big, mv_big]
    in_specs = [tokens(D_MODEL), full((1, D_MODEL)), _layer_spec(w_in, j), full((1, SB_HEAD_DIM)),
                full((1, MEM_WIDTH)), _layer_spec(mk_big, layer), _layer_spec(mv_big, layer)]
    out_specs = [tokens(SB_WIDTH), tokens(MEM_WIDTH)]
    out_shape = [jax.ShapeDtypeStruct((t, SB_WIDTH), BF16), jax.ShapeDtypeStruct((t, MEM_WIDTH), BF16)]
    if kv is not None:
        kvg, w_kv, kg = kv
        args += [kvg, w_kv, kg]
        in_specs += [full((1, D_MODEL)), full(w_kv.shape), full((1, SB_HEAD_DIM))]
        out_specs += [tokens(SB_WIDTH)] * 2
        out_shape += [jax.ShapeDtypeStruct((t, SB_WIDTH), BF16)] * 2
    return pl.pallas_call(
        functools.partial(_mixer_b_pre_kernel, with_kv=kv is not None),
        grid=(t // tm,),
        in_specs=in_specs,
        out_specs=out_specs,
        out_shape=out_shape,
        compiler_params=pltpu.CompilerParams(
            dimension_semantics=("arbitrary",), vmem_limit_bytes=VMEM_LIMIT),
        name="mixer_b_pre",
    )(*args)


def _sb_kernel(q_ref, k_ref, v_ref, o_ref):
    first_block = pl.program_id(1) * SB_GROUP
    row = lax.broadcasted_iota(jnp.int32, (SB_BLOCK, SB_BLOCK), 0)
    col = lax.broadcasted_iota(jnp.int32, (SB_BLOCK, SB_BLOCK), 1)
    causal = col < row
    suffix_ones = (row >= col).astype(BF16)

    def log2_one_minus_sigmoid(z):
        return jnp.minimum(-z, 0.0) - jnp.log(1.0 + jnp.exp2(-jnp.abs(z))) * LOG2_E

    def earlier_key_block(q, kb, carry, acc):
        start = pl.multiple_of(kb * SB_BLOCK, SB_BLOCK)
        k = k_ref[pl.ds(start, SB_BLOCK), :]
        v = v_ref[pl.ds(start, SB_BLOCK), :]
        z = lax.dot_general(q, k, (((1,), (1,)), ((), ())), preferred_element_type=F32)
        within = _dot(log2_one_minus_sigmoid(z).astype(BF16), suffix_ones)
        w = jnp.exp2(z + within + carry)
        acc = acc + _dot(w.astype(BF16), v)
        return carry + within[:, 0:1], acc

    def block_rows(g, n=1):
        return slice(g * SB_BLOCK, (g + n) * SB_BLOCK)

    def run(first_step):
        offsets = range(0 if first_step else 1 - SB_UNROLLED_BLOCKS, SB_GROUP)
        users = {off: range(max(off, 0), min(off + SB_UNROLLED_BLOCKS, SB_GROUP)) for off in offsets}
        tiles = [(g, g - off) for off in offsets for g in users[off]]

        z = {}
        for off in offsets:
            start = pl.multiple_of((first_block + off) * SB_BLOCK, SB_BLOCK)
            k = k_ref[pl.ds(start, SB_BLOCK), :]
            gs = users[off]
            zz = lax.dot_general(q_ref[block_rows(gs[0], len(gs)), :], k, (((1,), (1,)), ((), ())),
                                 preferred_element_type=F32)
            for n, g in enumerate(gs):
                z[g, g - off] = zz[block_rows(n), :]

        parts = []
        for g, back in tiles:
            lf = log2_one_minus_sigmoid(z[g, back])
            if back == 0:
                lf = jnp.where(causal, lf, 0.0)
            parts.append(lf.astype(BF16))
        within_all = _dot(jnp.concatenate(parts, axis=0), suffix_ones)
        within = {t: within_all[block_rows(n), :] for n, t in enumerate(tiles)}

        carries, w = [], {}
        for g in range(SB_GROUP):
            carry = jnp.zeros((SB_BLOCK, 1), F32)
            for back in range(SB_UNROLLED_BLOCKS):
                if (g, back) not in within:
                    continue
                wt = jnp.exp2(z[g, back] + within[g, back] + carry)
                if back == 0:
                    wt = jnp.where(causal, wt, 0.0)
                w[g, back] = wt.astype(BF16)
                carry = carry + within[g, back][:, 0:1]
            carries.append(carry)

        accs = [jnp.zeros((SB_BLOCK, SB_HEAD_DIM), F32) for _ in range(SB_GROUP)]
        for off in offsets:
            start = pl.multiple_of((first_block + off) * SB_BLOCK, SB_BLOCK)
            v = v_ref[pl.ds(start, SB_BLOCK), :]
            gs = users[off]
            pv = _dot(jnp.concatenate([w[g, g - off] for g in gs], axis=0), v)
            for n, g in enumerate(gs):
                accs[g] = accs[g] + pv[block_rows(n), :]
        for g in range(SB_GROUP):
            o_ref[block_rows(g), :] = accs[g].astype(o_ref.dtype)

        worst = functools.reduce(jnp.maximum, carries)

        @pl.when(jnp.max(worst) > SB_ZERO_LOG2)
        def _():
            for g in range(SB_GROUP):
                q = q_ref[block_rows(g), :]

                def cond(state):
                    kb, carry, _ = state
                    return jnp.logical_and(kb >= 0, jnp.max(carry) > SB_ZERO_LOG2)

                def body(state):
                    kb, carry, acc = state
                    carry, acc = earlier_key_block(q, kb, carry, acc)
                    return kb - 1, carry, acc

                start = first_block + g - SB_UNROLLED_BLOCKS
                _, _, acc = lax.while_loop(cond, body, (start, carries[g], accs[g]))
                o_ref[block_rows(g), :] = acc.astype(o_ref.dtype)

    assert SB_GROUP >= SB_UNROLLED_BLOCKS - 1
    pl.when(first_block == 0)(lambda: run(True))
    pl.when(first_block > 0)(lambda: run(False))


def _sb_attention(q, k, v):
    t = q.shape[0]
    tq = SB_GROUP * SB_BLOCK
    return pl.pallas_call(
        _sb_kernel,
        grid=(SB_HEADS, t // tq),
        in_specs=[
            pl.BlockSpec((tq, SB_HEAD_DIM), lambda h, i: (i, h)),
            pl.BlockSpec((t, SB_HEAD_DIM), lambda h, i: (0, h)),
            pl.BlockSpec((t, SB_HEAD_DIM), lambda h, i: (0, h)),
        ],
        out_specs=pl.BlockSpec((tq, SB_HEAD_DIM), lambda h, i: (i, h)),
        out_shape=jax.ShapeDtypeStruct((t, SB_WIDTH), BF16),
        compiler_params=pltpu.CompilerParams(
            dimension_semantics=("arbitrary", "arbitrary"), vmem_limit_bytes=VMEM_LIMIT),
        name="sb_attention",
    )(q, k, v)


def _token_tile(t, tm):
    while t % tm:
        tm //= 2
    return tm


def kernel(x, mem, mem_norm_g, mix_norm_g, ffn_norm_g, w_in_a, conv_w, conv_ln_g, conv_ln_b, w_out_a, w_in_b, sb_q_norm_g, w_out_b, kv_norm_g, w_kv, sb_k_norm_g, w_mem_kv, mem_q_norm_g, mem_k_norm_g, router_g_w, router_g_b, router_e_w, router_e_b, moe_w_in, moe_w_down):
    b, t, d = x.shape
    assert b == 1 and d == D_MODEL and t % (SB_GROUP * SB_BLOCK) == 0
    tm = _token_tile(t, TOKEN_TILE)
    tm_moe = _token_tile(t, MOE_TOKEN_TILE)
    xt = x.reshape(t, d)

    mk_big, mv_big = _memkv(mem.reshape(MEM_TOKENS, d), mem_norm_g, w_mem_kv, mem_k_norm_g)
    mem_qg = jnp.tile(mem_q_norm_g, (1, MEM_HEADS)).reshape(DEPTH, 1, MEM_WIDTH)

    pad = ROUTER_LANES - N_GROUPS - N_EXPERTS
    w_router = jnp.concatenate(
        [router_g_w, router_e_w.transpose(0, 2, 1, 3).reshape(DEPTH, d, N_EXPERTS),
         jnp.zeros((DEPTH, d, pad), F32)], axis=-1)
    b_router = jnp.concatenate(
        [router_g_b, router_e_b.reshape(DEPTH, N_EXPERTS), jnp.zeros((DEPTH, pad), F32)],
        axis=-1).reshape(DEPTH, 1, ROUTER_LANES)
    moe_in = moe_w_in.astype(BF16).reshape(DEPTH, N_EXPERTS, d, 2 * EXPERT_HIDDEN)
    moe_down = moe_w_down.astype(BF16).reshape(DEPTH, N_EXPERTS * EXPERT_HIDDEN, d)

    w_in_a, w_out_a, w_in_b, w_out_b = (w.astype(BF16) for w in (w_in_a, w_out_a, w_in_b, w_out_b))
    shared_kv = (kv_norm_g.reshape(1, d), w_kv.astype(BF16), sb_k_norm_g.reshape(1, SB_HEAD_DIM))

    sb_k = sb_v = None
    for l in range(DEPTH):
        g_mix = mix_norm_g[l].reshape(1, d)
        out_proj = None
        if l < N_A_LAYERS:
            xt = _mixer_a(xt, g_mix, w_in_a, conv_w,
                          conv_ln_g[l].reshape(1, CONV_CH), conv_ln_b[l].reshape(1, CONV_CH),
                          mem_qg[l], mk_big, mv_big, w_out_a, l, tm)
        else:
            j = l - N_A_LAYERS
            outs = _mixer_b_pre(xt, g_mix, w_in_b, sb_q_norm_g[j].reshape(1, SB_HEAD_DIM),
                                mem_qg[l], mk_big, mv_big, j, l, tm,
                                kv=shared_kv if j == 0 else None)
            if j == 0:
                q, mo, sb_k, sb_v = outs
            else:
                q, mo = outs
            out_proj = (_sb_attention(q, sb_k, sb_v), mo, w_out_b, j)
        xt = _moe(xt, ffn_norm_g[l].reshape(1, d), w_router, b_router, moe_in, moe_down, l, tm_moe,
                  out_proj=out_proj)
    return xt.reshape(b, t, d)
```

```python
import functools

import jax
import jax.numpy as jnp
from jax import lax
from jax.experimental import pallas as pl
from jax.experimental.pallas import tpu as pltpu

F32 = jnp.float32
BF16 = jnp.bfloat16

D_MODEL = 1024
DEPTH = 4
N_A_LAYERS = DEPTH // 2
MEM_TOKENS = 256
MEM_HEADS = 4
MEM_HEAD_DIM = 64
MEM_WIDTH = MEM_HEADS * MEM_HEAD_DIM
CONV_CH = D_MODEL - MEM_WIDTH
CONV_WIDTH = 31
SB_HEADS = 4
SB_HEAD_DIM = 128
SB_WIDTH = SB_HEADS * SB_HEAD_DIM
N_GROUPS = 4
EXPERTS_PER_GROUP = 4
N_EXPERTS = N_GROUPS * EXPERTS_PER_GROUP
EXPERT_HIDDEN = D_MODEL // 8
EPS = 1e-6

LANES = 128
SUBLANES = 8
CONV_HALO = 32
CONV_ROWS = 128
CONV_PAD = 2 * SUBLANES
SB_BLOCK = 128
SB_GROUP = 16
SB_UNROLLED_BLOCKS = 3
LOG2_E = 1.4426950408889634
SB_ZERO_LOG2 = -106.0 * LOG2_E
VMEM_LIMIT = 56 * 1024 * 1024
TOKEN_TILE = 512
MOE_TOKEN_TILE = 1024


def _dot(a, b):
    return jnp.dot(a, b, preferred_element_type=F32)


def _split(a):
    hi = a.astype(BF16)
    lo = (a - hi.astype(F32)).astype(BF16)
    return hi, lo


def _dot_exact_rhs(a, b):
    hi, lo = _split(a)
    return _dot(hi, b) + _dot(lo, b)


def _dot3(a, b):
    ah, al = _split(a)
    bh, bl = _split(b)
    n = b.shape[1]
    both = _dot(ah, jnp.concatenate([bh, bl], axis=1))
    return both[:, :n] + (both[:, n:] + _dot(al, bh))


def _rms(x, g):
    ms = jnp.mean(x * x, axis=-1, keepdims=True)
    return x * lax.rsqrt(ms + EPS) * g


def _sigmoid(x):
    return 0.5 * jnp.tanh(0.5 * x) + 0.5


def _full_spec(shape):
    return pl.BlockSpec(shape, lambda *_: (0,) * len(shape), pipeline_mode=pl.Buffered(1))


def _layer_spec(stacked, layer):
    rest = stacked.shape[1:]
    return pl.BlockSpec((None,) + rest, lambda *_: (layer,) + (0,) * len(rest),
                        pipeline_mode=pl.Buffered(1))


def _head_block_ones(n, head_dim):
    r = lax.broadcasted_iota(jnp.int32, (n, n), 0) // head_dim
    c = lax.broadcasted_iota(jnp.int32, (n, n), 1) // head_dim
    return (r == c).astype(BF16)


def _memkv_kernel(mem_ref, g_ref, w_ref, kg_ref, mk_ref, mv_ref):
    mem_n = _rms(mem_ref[...], g_ref[...]).astype(BF16)
    kv = _dot(mem_n, w_ref[...])
    mk = kv[:, :MEM_WIDTH]
    mv = kv[:, MEM_WIDTH:]
    ss = _dot_exact_rhs(mk * mk, _head_block_ones(MEM_WIDTH, MEM_HEAD_DIM))
    mk = mk * lax.rsqrt(ss * (1.0 / MEM_HEAD_DIM) + EPS) * (kg_ref[...] * MEM_HEAD_DIM ** -0.5)
    mk_t = mk.T
    mk_ref[...] = jnp.zeros(mk_ref.shape, mk_ref.dtype)
    lane_head = lax.broadcasted_iota(jnp.int32, mv.shape, 1) // MEM_HEAD_DIM
    for h in range(MEM_HEADS):
        rows = slice(h * MEM_HEAD_DIM, (h + 1) * MEM_HEAD_DIM)
        mk_ref[rows, h * MEM_TOKENS:(h + 1) * MEM_TOKENS] = mk_t[rows, :].astype(BF16)
        mv_ref[h * MEM_TOKENS:(h + 1) * MEM_TOKENS, :] = jnp.where(lane_head == h, mv, 0.0).astype(BF16)


def _memkv(mem, mem_norm_g, w_mem_kv, mem_k_norm_g):
    kg = jnp.tile(mem_k_norm_g, (1, MEM_HEADS)).reshape(DEPTH, 1, MEM_WIDTH)
    return pl.pallas_call(
        _memkv_kernel,
        grid=(DEPTH,),
        in_specs=[
            pl.BlockSpec((MEM_TOKENS, D_MODEL), lambda l: (0, 0)),
            pl.BlockSpec((1, D_MODEL), lambda l: (0, 0)),
            pl.BlockSpec((None, D_MODEL, 2 * MEM_WIDTH), lambda l: (l, 0, 0)),
            pl.BlockSpec((None, 1, MEM_WIDTH), lambda l: (l, 0, 0)),
        ],
        out_specs=[
            pl.BlockSpec((None, MEM_WIDTH, MEM_HEADS * MEM_TOKENS), lambda l: (l, 0, 0)),
            pl.BlockSpec((None, MEM_HEADS * MEM_TOKENS, MEM_WIDTH), lambda l: (l, 0, 0)),
        ],
        out_shape=[
            jax.ShapeDtypeStruct((DEPTH, MEM_WIDTH, MEM_HEADS * MEM_TOKENS), BF16),
            jax.ShapeDtypeStruct((DEPTH, MEM_HEADS * MEM_TOKENS, MEM_WIDTH), BF16),
        ],
        name="memkv",
    )(mem, mem_norm_g.reshape(1, D_MODEL), w_mem_kv.astype(BF16), kg)


def _mem_attention(q, qg, mk_big, mv_big):
    ss = _dot_exact_rhs(q * q, _head_block_ones(MEM_WIDTH, MEM_HEAD_DIM))
    qn = q * lax.rsqrt(ss * (1.0 / MEM_HEAD_DIM) + EPS) * qg
    s = _dot(qn.astype(BF16), mk_big)
    probs = []
    for h in range(MEM_HEADS):
        sh = s[:, h * MEM_TOKENS:(h + 1) * MEM_TOKENS]
        e = jnp.exp(sh - jnp.max(sh, axis=-1, keepdims=True))
        probs.append((e * (1.0 / jnp.sum(e, axis=-1, keepdims=True))).astype(BF16))
    return _dot(jnp.concatenate(probs, axis=1), mv_big)


def _mixer_a_kernel(x_ref, g_ref, win_ref, cw_ref, lng_ref, lnb_ref, qg_ref, mk_ref, mv_ref,
                    wout_ref, o_ref, cbuf_ref, conv_ref, xprev_ref, moprev_ref):
    tm = x_ref.shape[0]

    @pl.when(pl.program_id(0) == 0)
    def _():
        for ref in (cbuf_ref, xprev_ref, moprev_ref):
            ref[...] = jnp.zeros(ref.shape, ref.dtype)

    x = x_ref[...]
    h = _rms(x, g_ref[...]).astype(BF16)
    u = _dot(h, win_ref[...])

    base = CONV_HALO - (CONV_WIDTH - 1)
    win = CONV_ROWS + SUBLANES
    for r0 in range(0, tm, CONV_ROWS):
        for c0 in range(0, CONV_CH, LANES):
            cols = slice(c0, c0 + LANES)
            acc = None
            for shift in range(SUBLANES):
                part = None
                for k in range(CONV_WIDTH):
                    if (base + k) % SUBLANES != shift:
                        continue
                    start = r0 + (base + k - shift)
                    term = cw_ref[k:k + 1, cols] * cbuf_ref[start:start + win, cols]
                    part = term if part is None else part + term
                part = part[shift:shift + CONV_ROWS, :]
                acc = part if acc is None else acc + part
            conv_ref[r0:r0 + CONV_ROWS, cols] = acc

    cbuf_ref[0:CONV_HALO, :] = cbuf_ref[tm:tm + CONV_HALO, :]
    cbuf_ref[CONV_HALO:CONV_HALO + tm, :] = u[:, :CONV_CH] * _sigmoid(u[:, CONV_CH:2 * CONV_CH])

    c = conv_ref[...]
    xc = c - jnp.mean(c, axis=-1, keepdims=True)
    var = jnp.mean(xc * xc, axis=-1, keepdims=True)
    y = xc * lax.rsqrt(var + EPS) * lng_ref[...] + lnb_ref[...]
    mixed = (y * _sigmoid(y)).astype(BF16)
    o_ref[...] = (xprev_ref[...] + _dot(mixed, wout_ref[0:CONV_CH, :])
                  + _dot(moprev_ref[...], wout_ref[CONV_CH:, :]))

    xprev_ref[...] = x
    moprev_ref[...] = _mem_attention(u[:, 2 * CONV_CH:], qg_ref[...], mk_ref[...],
                                     mv_ref[...]).astype(BF16)


def _mixer_a(x, g, w_in, conv_w, ln_g, ln_b, qg, mk_big, mv_big, w_out, layer, tm):
    t = x.shape[0]
    n = t // tm
    full = _full_spec
    return pl.pallas_call(
        _mixer_a_kernel,
        grid=(n + 1,),
        in_specs=[
            pl.BlockSpec((tm, D_MODEL), lambda i: (jnp.minimum(i, n - 1), 0)),
            full((1, D_MODEL)),
            _layer_spec(w_in, layer),
            _layer_spec(conv_w, layer),
            full((1, CONV_CH)),
            full((1, CONV_CH)),
            full((1, MEM_WIDTH)),
            _layer_spec(mk_big, layer),
            _layer_spec(mv_big, layer),
            _layer_spec(w_out, layer),
        ],
        out_specs=pl.BlockSpec((tm, D_MODEL), lambda i: (jnp.maximum(i - 1, 0), 0)),
        out_shape=jax.ShapeDtypeStruct((t, D_MODEL), F32),
        scratch_shapes=[
            pltpu.VMEM((tm + CONV_HALO + CONV_PAD, CONV_CH), F32),
            pltpu.VMEM((tm, CONV_CH), F32),
            pltpu.VMEM((tm, D_MODEL), F32),
            pltpu.VMEM((tm, MEM_WIDTH), BF16),
        ],
        compiler_params=pltpu.CompilerParams(
            dimension_semantics=("arbitrary",), vmem_limit_bytes=VMEM_LIMIT),
        name="mixer_a",
    )(x, g, w_in, conv_w, ln_g, ln_b, qg, mk_big, mv_big, w_out)


ROUTER_LANES = LANES
EXPERT_LANE0 = N_GROUPS


def _route(logits):
    neg = jnp.float32(-jnp.inf)
    lane = lax.broadcasted_iota(jnp.int32, logits.shape, 1)
    lane_f = lane.astype(F32)
    is_g = lane < N_GROUPS
    gl = jnp.where(is_g, logits, neg)
    gmax = jnp.max(gl, axis=-1, keepdims=True)
    gidx = jnp.min(jnp.where(gl == gmax, lane_f, float(ROUTER_LANES)), axis=-1, keepdims=True)
    gsum = jnp.sum(jnp.where(is_g, jnp.exp(gl - gmax), 0.0), axis=-1, keepdims=True)
    g_gate = 1.0 / gsum
    lane_group = ((lane - EXPERT_LANE0) // EXPERTS_PER_GROUP).astype(F32)
    sel = (lane >= EXPERT_LANE0) & (lane < EXPERT_LANE0 + N_EXPERTS) & (lane_group == gidx)
    sl = jnp.where(sel, logits, neg)
    m1 = jnp.max(sl, axis=-1, keepdims=True)
    i1 = jnp.min(jnp.where(sl == m1, lane_f, float(ROUTER_LANES)), axis=-1, keepdims=True)
    sl2 = jnp.where(lane_f == i1, neg, sl)
    m2 = jnp.max(sl2, axis=-1, keepdims=True)
    i2 = jnp.min(jnp.where(sl2 == m2, lane_f, float(ROUTER_LANES)), axis=-1, keepdims=True)
    e2 = jnp.exp(m2 - m1)
    w1 = g_gate / (1.0 + e2)
    w2 = w1 * e2
    return jnp.where(lane_f == i1, w1, jnp.where(lane_f == i2, w2, 0.0))


def _moe_kernel(*refs, with_out_proj):
    if with_out_proj:
        x_ref, sb_ref, mo_ref, wout_ref, g_ref, wr_ref, br_ref, win_ref, wdn_ref, o_ref, act_ref = refs
        x = (x_ref[...] + _dot(sb_ref[...], wout_ref[0:SB_WIDTH, :])
             + _dot(mo_ref[...], wout_ref[SB_WIDTH:, :]))
    else:
        x_ref, g_ref, wr_ref, br_ref, win_ref, wdn_ref, o_ref, act_ref = refs
        x = x_ref[...]
    hf = _rms(x, g_ref[...])
    hb = hf.astype(BF16)
    gate = _route(_dot3(hf, wr_ref[...]) + br_ref[...])
    for e in range(N_EXPERTS):
        hu = _dot(hb, win_ref[e])
        a = hu[:, :EXPERT_HIDDEN]
        act = a * _sigmoid(a) * hu[:, EXPERT_HIDDEN:] * gate[:, EXPERT_LANE0 + e:EXPERT_LANE0 + e + 1]
        act_ref[:, e * EXPERT_HIDDEN:(e + 1) * EXPERT_HIDDEN] = act.astype(BF16)
    o_ref[...] = x + _dot(act_ref[...], wdn_ref[...])


def _moe(x, g, w_router, b_router, w_in, w_down, layer, tm, out_proj=None):
    t = x.shape[0]
    tokens = lambda width: pl.BlockSpec((tm, width), lambda i: (i, 0))
    args, in_specs = [x], [tokens(D_MODEL)]
    if out_proj is not None:
        sb, mo, w_out, j = out_proj
        args += [sb, mo, w_out]
        in_specs += [tokens(SB_WIDTH), tokens(MEM_WIDTH), _layer_spec(w_out, j)]
    args += [g, w_router, b_router, w_in, w_down]
    in_specs += [_full_spec((1, D_MODEL)), _layer_spec(w_router, layer), _layer_spec(b_router, layer),
                 _layer_spec(w_in, layer), _layer_spec(w_down, layer)]
    return pl.pallas_call(
        functools.partial(_moe_kernel, with_out_proj=out_proj is not None),
        grid=(t // tm,),
        in_specs=in_specs,
        out_specs=tokens(D_MODEL),
        out_shape=jax.ShapeDtypeStruct((t, D_MODEL), F32),
        scratch_shapes=[pltpu.VMEM((tm, N_EXPERTS * EXPERT_HIDDEN), BF16)],
        compiler_params=pltpu.CompilerParams(
            dimension_semantics=("arbitrary",), vmem_limit_bytes=VMEM_LIMIT),
        name="moe",
    )(*args)


def _sb_queries(u, sqg_ref, q_ref):
    sqg = sqg_ref[...] * (SB_HEAD_DIM ** -0.5 * LOG2_E)
    for hd in range(SB_HEADS):
        cols = slice(hd * SB_HEAD_DIM, (hd + 1) * SB_HEAD_DIM)
        q_ref[:, cols] = _rms(u[:, cols], sqg).astype(BF16)


def _mixer_b_pre_kernel(*refs, with_kv):
    if with_kv:
        (x_ref, g_ref, win_ref, sqg_ref, qg_ref, mk_ref, mv_ref, kvg_ref, wkv_ref, kg_ref,
         q_ref, mo_ref, k_ref, v_ref) = refs
    else:
        x_ref, g_ref, win_ref, sqg_ref, qg_ref, mk_ref, mv_ref, q_ref, mo_ref = refs
    x = x_ref[...]
    unit = x * lax.rsqrt(jnp.mean(x * x, axis=-1, keepdims=True) + EPS)
    u = _dot((unit * g_ref[...]).astype(BF16), win_ref[...])
    _sb_queries(u, sqg_ref, q_ref)
    mo_ref[...] = _mem_attention(u[:, SB_WIDTH:], qg_ref[...], mk_ref[...], mv_ref[...]).astype(BF16)
    if with_kv:
        kv = _dot((unit * kvg_ref[...]).astype(BF16), wkv_ref[...])
        for h in range(SB_HEADS):
            cols = slice(h * SB_HEAD_DIM, (h + 1) * SB_HEAD_DIM)
            k_ref[:, cols] = _rms(kv[:, cols], kg_ref[...]).astype(BF16)
        v_ref[...] = kv[:, SB_WIDTH:].astype(BF16)


def _mixer_b_pre(x, g, w_in, sqg, qg, mk_big, mv_big, j, layer, tm, kv=None):
    t = x.shape[0]
    full = _full_spec
    tokens = lambda width: pl.BlockSpec((tm, width), lambda i: (i, 0))
    args = [x, g, w_in, sqg, qg, mk_big, mv_big]
    in_specs = [tokens(D_MODEL), full((1, D_MODEL)), _layer_spec(w_in, j), full((1, SB_HEAD_DIM)),
                full((1, MEM_WIDTH)), _layer_spec(mk_big, layer), _layer_spec(mv_big, layer)]
    out_specs = [tokens(SB_WIDTH), tokens(MEM_WIDTH)]
    out_shape = [jax.ShapeDtypeStruct((t, SB_WIDTH), BF16), jax.ShapeDtypeStruct((t, MEM_WIDTH), BF16)]
    if kv is not None:
        kvg, w_kv, kg = kv
        args += [kvg, w_kv, kg]
        in_specs += [full((1, D_MODEL)), full(w_kv.shape), full((1, SB_HEAD_DIM))]
        out_specs += [tokens(SB_WIDTH)] * 2
        out_shape += [jax.ShapeDtypeStruct((t, SB_WIDTH), BF16)] * 2
    return pl.pallas_call(
        functools.partial(_mixer_b_pre_kernel, with_kv=kv is not None),
        grid=(t // tm,),
        in_specs=in_specs,
        out_specs=out_specs,
        out_shape=out_shape,
        compiler_params=pltpu.CompilerParams(
            dimension_semantics=("arbitrary",), vmem_limit_bytes=VMEM_LIMIT),
        name="mixer_b_pre",
    )(*args)


def _sb_kernel(q_ref, k_ref, v_ref, o_ref):
    first_block = pl.program_id(1) * SB_GROUP
    row = lax.broadcasted_iota(jnp.int32, (SB_BLOCK, SB_BLOCK), 0)
    col = lax.broadcasted_iota(jnp.int32, (SB_BLOCK, SB_BLOCK), 1)
    causal = col < row
    suffix_ones = (row >= col).astype(BF16)

    def log2_one_minus_sigmoid(z):
        return jnp.minimum(-z, 0.0) - jnp.log(1.0 + jnp.exp2(-jnp.abs(z))) * LOG2_E

    def earlier_key_block(q, kb, carry, acc):
        start = pl.multiple_of(kb * SB_BLOCK, SB_BLOCK)
        k = k_ref[pl.ds(start, SB_BLOCK), :]
        v = v_ref[pl.ds(start, SB_BLOCK), :]
        z = lax.dot_general(q, k, (((1,), (1,)), ((), ())), preferred_element_type=F32)
        within = _dot(log2_one_minus_sigmoid(z).astype(BF16), suffix_ones)
        w = jnp.exp2(z + within + carry)
        acc = acc + _dot(w.astype(BF16), v)
        return carry + within[:, 0:1], acc

    def block_rows(g, n=1):
        return slice(g * SB_BLOCK, (g + n) * SB_BLOCK)

    def run(first_step):
        offsets = range(0 if first_step else 1 - SB_UNROLLED_BLOCKS, SB_GROUP)
        users = {off: range(max(off, 0), min(off + SB_UNROLLED_BLOCKS, SB_GROUP)) for off in offsets}
        tiles = [(g, g - off) for off in offsets for g in users[off]]

        z = {}
        for off in offsets:
            start = pl.multiple_of((first_block + off) * SB_BLOCK, SB_BLOCK)
            k = k_ref[pl.ds(start, SB_BLOCK), :]
            gs = users[off]
            zz = lax.dot_general(q_ref[block_rows(gs[0], len(gs)), :], k, (((1,), (1,)), ((), ())),
                                 preferred_element_type=F32)
            for n, g in enumerate(gs):
                z[g, g - off] = zz[block_rows(n), :]

        parts = []
        for g, back in tiles:
            lf = log2_one_minus_sigmoid(z[g, back])
            if back == 0:
                lf = jnp.where(causal, lf, 0.0)
            parts.append(lf.astype(BF16))
        within_all = _dot(jnp.concatenate(parts, axis=0), suffix_ones)
        within = {t: within_all[block_rows(n), :] for n, t in enumerate(tiles)}

        carries, w = [], {}
        for g in range(SB_GROUP):
            carry = jnp.zeros((SB_BLOCK, 1), F32)
            for back in range(SB_UNROLLED_BLOCKS):
                if (g, back) not in within:
                    continue
                wt = jnp.exp2(z[g, back] + within[g, back] + carry)
                if back == 0:
                    wt = jnp.where(causal, wt, 0.0)
                w[g, back] = wt.astype(BF16)
                carry = carry + within[g, back][:, 0:1]
            carries.append(carry)

        accs = [jnp.zeros((SB_BLOCK, SB_HEAD_DIM), F32) for _ in range(SB_GROUP)]
        for off in offsets:
            start = pl.multiple_of((first_block + off) * SB_BLOCK, SB_BLOCK)
            v = v_ref[pl.ds(start, SB_BLOCK), :]
            gs = users[off]
            pv = _dot(jnp.concatenate([w[g, g - off] for g in gs], axis=0), v)
            for n, g in enumerate(gs):
                accs[g] = accs[g] + pv[block_rows(n), :]
        for g in range(SB_GROUP):
            o_ref[block_rows(g), :] = accs[g].astype(o_ref.dtype)

        worst = functools.reduce(jnp.maximum, carries)

        @pl.when(jnp.max(worst) > SB_ZERO_LOG2)
        def _():
            for g in range(SB_GROUP):
                q = q_ref[block_rows(g), :]

                def cond(state):
                    kb, carry, _ = state
                    return jnp.logical_and(kb >= 0, jnp.max(carry) > SB_ZERO_LOG2)

                def body(state):
                    kb, carry, acc = state
                    carry, acc = earlier_key_block(q, kb, carry, acc)
                    return kb - 1, carry, acc

                start = first_block + g - SB_UNROLLED_BLOCKS
                _, _, acc = lax.while_loop(cond, body, (start, carries[g], accs[g]))
                o_ref[block_rows(g), :] = acc.astype(o_ref.dtype)

    assert SB_GROUP >= SB_UNROLLED_BLOCKS - 1
    pl.when(first_block == 0)(lambda: run(True))
    pl.when(first_block > 0)(lambda: run(False))


def _sb_attention(q, k, v):
    t = q.shape[0]
    tq = SB_GROUP * SB_BLOCK
    return pl.pallas_call(
        _sb_kernel,
        grid=(SB_HEADS, t // tq),
        in_specs=[
            pl.BlockSpec((tq, SB_HEAD_DIM), lambda h, i: (i, h)),
            pl.BlockSpec((t, SB_HEAD_DIM), lambda h, i: (0, h)),
            pl.BlockSpec((t, SB_HEAD_DIM), lambda h, i: (0, h)),
        ],
        out_specs=pl.BlockSpec((tq, SB_HEAD_DIM), lambda h, i: (i, h)),
        out_shape=jax.ShapeDtypeStruct((t, SB_WIDTH), BF16),
        compiler_params=pltpu.CompilerParams(
            dimension_semantics=("arbitrary", "arbitrary"), vmem_limit_bytes=VMEM_LIMIT),
        name="sb_attention",
    )(q, k, v)


def _token_tile(t, tm):
    while t % tm:
        tm //= 2
    return tm


def kernel(x, mem, mem_norm_g, mix_norm_g, ffn_norm_g, w_in_a, conv_w, conv_ln_g, conv_ln_b, w_out_a, w_in_b, sb_q_norm_g, w_out_b, kv_norm_g, w_kv, sb_k_norm_g, w_mem_kv, mem_q_norm_g, mem_k_norm_g, router_g_w, router_g_b, router_e_w, router_e_b, moe_w_in, moe_w_down):
    b, t, d = x.shape
    assert b == 1 and d == D_MODEL and t % (SB_GROUP * SB_BLOCK) == 0
    tm = _token_tile(t, TOKEN_TILE)
    tm_moe = _token_tile(t, MOE_TOKEN_TILE)
    xt = x.reshape(t, d)

    mk_big, mv_big = _memkv(mem.reshape(MEM_TOKENS, d), mem_norm_g, w_mem_kv, mem_k_norm_g)
    mem_qg = jnp.tile(mem_q_norm_g, (1, MEM_HEADS)).reshape(DEPTH, 1, MEM_WIDTH)

    pad = ROUTER_LANES - N_GROUPS - N_EXPERTS
    w_router = jnp.concatenate(
        [router_g_w, router_e_w.transpose(0, 2, 1, 3).reshape(DEPTH, d, N_EXPERTS),
         jnp.zeros((DEPTH, d, pad), F32)], axis=-1)
    b_router = jnp.concatenate(
        [router_g_b, router_e_b.reshape(DEPTH, N_EXPERTS), jnp.zeros((DEPTH, pad), F32)],
        axis=-1).reshape(DEPTH, 1, ROUTER_LANES)
    moe_in = moe_w_in.astype(BF16).reshape(DEPTH, N_EXPERTS, d, 2 * EXPERT_HIDDEN)
    moe_down = moe_w_down.astype(BF16).reshape(DEPTH, N_EXPERTS * EXPERT_HIDDEN, d)

    w_in_a, w_out_a, w_in_b, w_out_b = (w.astype(BF16) for w in (w_in_a, w_out_a, w_in_b, w_out_b))
    shared_kv = (kv_norm_g.reshape(1, d), w_kv.astype(BF16), sb_k_norm_g.reshape(1, SB_HEAD_DIM))

    sb_k = sb_v = None
    for l in range(DEPTH):
        g_mix = mix_norm_g[l].reshape(1, d)
        out_proj = None
        if l < N_A_LAYERS:
            xt = _mixer_a(xt, g_mix, w_in_a, conv_w,
                          conv_ln_g[l].reshape(1, CONV_CH), conv_ln_b[l].reshape(1, CONV_CH),
                          mem_qg[l], mk_big, mv_big, w_out_a, l, tm)
        else:
            j = l - N_A_LAYERS
            outs = _mixer_b_pre(xt, g_mix, w_in_b, sb_q_norm_g[j].reshape(1, SB_HEAD_DIM),
                                mem_qg[l], mk_big, mv_big, j, l, tm,
                                kv=shared_kv if j == 0 else None)
            if j == 0:
                q, mo, sb_k, sb_v = outs
            else:
                q, mo = outs
            out_proj = (_sb_attention(q, sb_k, sb_v), mo, w_out_b, j)
        xt = _moe(xt, ffn_norm_g[l].reshape(1, d), w_router, b_router, moe_in, moe_down, l, tm_moe,
                  out_proj=out_proj)
    return xt.reshape(b, t, d)
```

```python
import functools

import jax
import jax.numpy as jnp
from jax import lax
from jax.experimental import pallas as pl
from jax.experimental.pallas import tpu as pltpu

F32 = jnp.float32
BF16 = jnp.bfloat16

D_MODEL = 1024
DEPTH = 4
N_A_LAYERS = DEPTH // 2
MEM_TOKENS = 256
MEM_HEADS = 4
MEM_HEAD_DIM = 64
MEM_WIDTH = MEM_HEADS * MEM_HEAD_DIM
CONV_CH = D_MODEL - MEM_WIDTH
CONV_WIDTH = 31
SB_HEADS = 4
SB_HEAD_DIM = 128
SB_WIDTH = SB_HEADS * SB_HEAD_DIM
N_GROUPS = 4
EXPERTS_PER_GROUP = 4
N_EXPERTS = N_GROUPS * EXPERTS_PER_GROUP
EXPERT_HIDDEN = D_MODEL // 8
EPS = 1e-6

LANES = 128
SUBLANES = 8
CONV_HALO = 32
CONV_ROWS = 128
CONV_PAD = 2 * SUBLANES
SB_BLOCK = 128
SB_GROUP = 16
SB_UNROLLED_BLOCKS = 3
LOG2_E = 1.4426950408889634
SB_ZERO_LOG2 = -106.0 * LOG2_E
VMEM_LIMIT = 56 * 1024 * 1024
TOKEN_TILE = 512
MOE_TOKEN_TILE = 1024


def _dot(a, b):
    return jnp.dot(a, b, preferred_element_type=F32)


def _split(a):
    hi = a.astype(BF16)
    lo = (a - hi.astype(F32)).astype(BF16)
    return hi, lo


def _dot_exact_rhs(a, b):
    hi, lo = _split(a)
    return _dot(hi, b) + _dot(lo, b)


def _dot3(a, b):
    ah, al = _split(a)
    bh, bl = _split(b)
    n = b.shape[1]
    both = _dot(ah, jnp.concatenate([bh, bl], axis=1))
    return both[:, :n] + (both[:, n:] + _dot(al, bh))


def _rms(x, g):
    ms = jnp.mean(x * x, axis=-1, keepdims=True)
    return x * lax.rsqrt(ms + EPS) * g


def _sigmoid(x):
    return 0.5 * jnp.tanh(0.5 * x) + 0.5


def _full_spec(shape):
    return pl.BlockSpec(shape, lambda *_: (0,) * len(shape), pipeline_mode=pl.Buffered(1))


def _layer_spec(stacked, layer):
    rest = stacked.shape[1:]
    return pl.BlockSpec((None,) + rest, lambda *_: (layer,) + (0,) * len(rest),
                        pipeline_mode=pl.Buffered(1))


def _bf16_copy_specs(stacked, layer, n):
    rows, cols = stacked.shape[1:]
    step_rows = rows // n
    assert step_rows * n == rows
    block = lambda i: jnp.minimum(i, n - 1)
    return (pl.BlockSpec((None, step_rows, cols), lambda i: (layer, block(i), 0)),
            pl.BlockSpec((step_rows, cols), lambda i: (block(i), 0)),
            jax.ShapeDtypeStruct((rows, cols), BF16))


def _head_block_ones(n, head_dim):
    r = lax.broadcasted_iota(jnp.int32, (n, n), 0) // head_dim
    c = lax.broadcasted_iota(jnp.int32, (n, n), 1) // head_dim
    return (r == c).astype(BF16)


def _memkv_kernel(mem_ref, g_ref, w_ref, kg_ref, mk_ref, mv_ref):
    mem_n = _rms(mem_ref[...], g_ref[...]).astype(BF16)
    kv = _dot(mem_n, w_ref[...])
    mk = kv[:, :MEM_WIDTH]
    mv = kv[:, MEM_WIDTH:]
    ss = _dot_exact_rhs(mk * mk, _head_block_ones(MEM_WIDTH, MEM_HEAD_DIM))
    mk = mk * lax.rsqrt(ss * (1.0 / MEM_HEAD_DIM) + EPS) * (kg_ref[...] * MEM_HEAD_DIM ** -0.5)
    mk_t = mk.T
    mk_ref[...] = jnp.zeros(mk_ref.shape, mk_ref.dtype)
    lane_head = lax.broadcasted_iota(jnp.int32, mv.shape, 1) // MEM_HEAD_DIM
    for h in range(MEM_HEADS):
        rows = slice(h * MEM_HEAD_DIM, (h + 1) * MEM_HEAD_DIM)
        mk_ref[rows, h * MEM_TOKENS:(h + 1) * MEM_TOKENS] = mk_t[rows, :].astype(BF16)
        mv_ref[h * MEM_TOKENS:(h + 1) * MEM_TOKENS, :] = jnp.where(lane_head == h, mv, 0.0).astype(BF16)


def _memkv(mem, mem_norm_g, w_mem_kv, mem_k_norm_g):
    kg = jnp.tile(mem_k_norm_g, (1, MEM_HEADS)).reshape(DEPTH, 1, MEM_WIDTH)
    return pl.pallas_call(
        _memkv_kernel,
        grid=(DEPTH,),
        in_specs=[
            pl.BlockSpec((MEM_TOKENS, D_MODEL), lambda l: (0, 0)),
            pl.BlockSpec((1, D_MODEL), lambda l: (0, 0)),
            pl.BlockSpec((None, D_MODEL, 2 * MEM_WIDTH), lambda l: (l, 0, 0)),
            pl.BlockSpec((None, 1, MEM_WIDTH), lambda l: (l, 0, 0)),
        ],
        out_specs=[
            pl.BlockSpec((None, MEM_WIDTH, MEM_HEADS * MEM_TOKENS), lambda l: (l, 0, 0)),
            pl.BlockSpec((None, MEM_HEADS * MEM_TOKENS, MEM_WIDTH), lambda l: (l, 0, 0)),
        ],
        out_shape=[
            jax.ShapeDtypeStruct((DEPTH, MEM_WIDTH, MEM_HEADS * MEM_TOKENS), BF16),
            jax.ShapeDtypeStruct((DEPTH, MEM_HEADS * MEM_TOKENS, MEM_WIDTH), BF16),
        ],
        name="memkv",
    )(mem, mem_norm_g.reshape(1, D_MODEL), w_mem_kv.astype(BF16), kg)


def _mem_attention(q, qg, mk_big, mv_big):
    ss = _dot_exact_rhs(q * q, _head_block_ones(MEM_WIDTH, MEM_HEAD_DIM))
    qn = q * lax.rsqrt(ss * (1.0 / MEM_HEAD_DIM) + EPS) * qg
    s = _dot(qn.astype(BF16), mk_big)
    probs = []
    for h in range(MEM_HEADS):
        sh = s[:, h * MEM_TOKENS:(h + 1) * MEM_TOKENS]
        e = jnp.exp(sh - jnp.max(sh, axis=-1, keepdims=True))
        probs.append((e * (1.0 / jnp.sum(e, axis=-1, keepdims=True))).astype(BF16))
    return _dot(jnp.concatenate(probs, axis=1), mv_big)


def _mixer_a_kernel(x_ref, g_ref, win_ref, cw_ref, lng_ref, lnb_ref, qg_ref, mk_ref, mv_ref,
                    wout_ref, ein_ref, edn_ref, o_ref, ein_bf_ref, edn_bf_ref,
                    cbuf_ref, conv_ref, xprev_ref, moprev_ref):
    tm = x_ref.shape[0]
    ein_bf_ref[...] = ein_ref[...].astype(BF16)
    edn_bf_ref[...] = edn_ref[...].astype(BF16)

    @pl.when(pl.program_id(0) == 0)
    def _():
        for ref in (cbuf_ref, xprev_ref, moprev_ref):
            ref[...] = jnp.zeros(ref.shape, ref.dtype)

    x = x_ref[...]
    h = _rms(x, g_ref[...]).astype(BF16)
    u = _dot(h, win_ref[...])

    base = CONV_HALO - (CONV_WIDTH - 1)
    win = CONV_ROWS + SUBLANES
    for r0 in range(0, tm, CONV_ROWS):
        for c0 in range(0, CONV_CH, LANES):
            cols = slice(c0, c0 + LANES)
            acc = None
            for shift in range(SUBLANES):
                part = None
                for k in range(CONV_WIDTH):
                    if (base + k) % SUBLANES != shift:
                        continue
                    start = r0 + (base + k - shift)
                    term = cw_ref[k:k + 1, cols] * cbuf_ref[start:start + win, cols]
                    part = term if part is None else part + term
                part = part[shift:shift + CONV_ROWS, :]
                acc = part if acc is None else acc + part
            conv_ref[r0:r0 + CONV_ROWS, cols] = acc

    cbuf_ref[0:CONV_HALO, :] = cbuf_ref[tm:tm + CONV_HALO, :]
    cbuf_ref[CONV_HALO:CONV_HALO + tm, :] = u[:, :CONV_CH] * _sigmoid(u[:, CONV_CH:2 * CONV_CH])

    c = conv_ref[...]
    xc = c - jnp.mean(c, axis=-1, keepdims=True)
    var = jnp.mean(xc * xc, axis=-1, keepdims=True)
    y = xc * lax.rsqrt(var + EPS) * lng_ref[...] + lnb_ref[...]
    mixed = (y * _sigmoid(y)).astype(BF16)
    o_ref[...] = (xprev_ref[...] + _dot(mixed, wout_ref[0:CONV_CH, :])
                  + _dot(moprev_ref[...], wout_ref[CONV_CH:, :]))

    xprev_ref[...] = x
    moprev_ref[...] = _mem_attention(u[:, 2 * CONV_CH:], qg_ref[...], mk_ref[...],
                                     mv_ref[...]).astype(BF16)


def _mixer_a(x, g, w_in, conv_w, ln_g, ln_b, qg, mk_big, mv_big, w_out, moe_in, moe_down, layer, tm):
    t = x.shape[0]
    n = t // tm
    full = _full_spec
    ein_in, ein_out, ein_shape = _bf16_copy_specs(moe_in, layer, n)
    edn_in, edn_out, edn_shape = _bf16_copy_specs(moe_down, layer, n)
    return pl.pallas_call(
        _mixer_a_kernel,
        grid=(n + 1,),
        in_specs=[
            pl.BlockSpec((tm, D_MODEL), lambda i: (jnp.minimum(i, n - 1), 0)),
            full((1, D_MODEL)),
            _layer_spec(w_in, layer),
            _layer_spec(conv_w, layer),
            full((1, CONV_CH)),
            full((1, CONV_CH)),
            full((1, MEM_WIDTH)),
            _layer_spec(mk_big, layer),
            _layer_spec(mv_big, layer),
            _layer_spec(w_out, layer),
            ein_in,
            edn_in,
        ],
        out_specs=[pl.BlockSpec((tm, D_MODEL), lambda i: (jnp.maximum(i - 1, 0), 0)), ein_out, edn_out],
        out_shape=[jax.ShapeDtypeStruct((t, D_MODEL), F32), ein_shape, edn_shape],
        scratch_shapes=[
            pltpu.VMEM((tm + CONV_HALO + CONV_PAD, CONV_CH), F32),
            pltpu.VMEM((tm, CONV_CH), F32),
            pltpu.VMEM((tm, D_MODEL), F32),
            pltpu.VMEM((tm, MEM_WIDTH), BF16),
        ],
        compiler_params=pltpu.CompilerParams(
            dimension_semantics=("arbitrary",), vmem_limit_bytes=VMEM_LIMIT),
        name="mixer_a",
    )(x, g, w_in, conv_w, ln_g, ln_b, qg, mk_big, mv_big, w_out, moe_in, moe_down)


ROUTER_LANES = LANES
EXPERT_LANE0 = N_GROUPS


def _route(logits):
    neg = jnp.float32(-jnp.inf)
    lane = lax.broadcasted_iota(jnp.int32, logits.shape, 1)
    lane_f = lane.astype(F32)
    is_g = lane < N_GROUPS
    gl = jnp.where(is_g, logits, neg)
    gmax = jnp.max(gl, axis=-1, keepdims=True)
    gidx = jnp.min(jnp.where(gl == gmax, lane_f, float(ROUTER_LANES)), axis=-1, keepdims=True)
    gsum = jnp.sum(jnp.where(is_g, jnp.exp(gl - gmax), 0.0), axis=-1, keepdims=True)
    g_gate = 1.0 / gsum
    lane_group = ((lane - EXPERT_LANE0) // EXPERTS_PER_GROUP).astype(F32)
    sel = (lane >= EXPERT_LANE0) & (lane < EXPERT_LANE0 + N_EXPERTS) & (lane_group == gidx)
    sl = jnp.where(sel, logits, neg)
    m1 = jnp.max(sl, axis=-1, keepdims=True)
    i1 = jnp.min(jnp.where(sl == m1, lane_f, float(ROUTER_LANES)), axis=-1, keepdims=True)
    sl2 = jnp.where(lane_f == i1, neg, sl)
    m2 = jnp.max(sl2, axis=-1, keepdims=True)
    i2 = jnp.min(jnp.where(sl2 == m2, lane_f, float(ROUTER_LANES)), axis=-1, keepdims=True)
    e2 = jnp.exp(m2 - m1)
    w1 = g_gate / (1.0 + e2)
    w2 = w1 * e2
    return jnp.where(lane_f == i1, w1, jnp.where(lane_f == i2, w2, 0.0))


def _moe_kernel(*refs, with_out_proj):
    if with_out_proj:
        x_ref, sb_ref, mo_ref, wout_ref, g_ref, wr_ref, br_ref, win_ref, wdn_ref, o_ref, act_ref = refs
        x = (x_ref[...] + _dot(sb_ref[...], wout_ref[0:SB_WIDTH, :])
             + _dot(mo_ref[...], wout_ref[SB_WIDTH:, :]))
    else:
        x_ref, g_ref, wr_ref, br_ref, win_ref, wdn_ref, o_ref, act_ref = refs
        x = x_ref[...]
    hf = _rms(x, g_ref[...])
    hb = hf.astype(BF16)
    gate = _route(_dot3(hf, wr_ref[...]) + br_ref[...])
    for e in range(N_EXPERTS):
        hu = _dot(hb, win_ref[e])
        a = hu[:, :EXPERT_HIDDEN]
        act = a * _sigmoid(a) * hu[:, EXPERT_HIDDEN:] * gate[:, EXPERT_LANE0 + e:EXPERT_LANE0 + e + 1]
        act_ref[:, e * EXPERT_HIDDEN:(e + 1) * EXPERT_HIDDEN] = act.astype(BF16)
    o_ref[...] = x + _dot(act_ref[...], wdn_ref[...])


def _moe(x, g, w_router, b_router, w_in, w_down, layer, tm, out_proj=None):
    t = x.shape[0]
    tokens = lambda width: pl.BlockSpec((tm, width), lambda i: (i, 0))
    args, in_specs = [x], [tokens(D_MODEL)]
    if out_proj is not None:
        sb, mo, w_out, j = out_proj
        args += [sb, mo, w_out]
        in_specs += [tokens(SB_WIDTH), tokens(MEM_WIDTH), _layer_spec(w_out, j)]
    args += [g, w_router, b_router, w_in, w_down]
    in_specs += [_full_spec((1, D_MODEL)), _layer_spec(w_router, layer), _layer_spec(b_router, layer),
                 _full_spec(w_in.shape), _full_spec(w_down.shape)]
    return pl.pallas_call(
        functools.partial(_moe_kernel, with_out_proj=out_proj is not None),
        grid=(t // tm,),
        in_specs=in_specs,
        out_specs=tokens(D_MODEL),
        out_shape=jax.ShapeDtypeStruct((t, D_MODEL), F32),
        scratch_shapes=[pltpu.VMEM((tm, N_EXPERTS * EXPERT_HIDDEN), BF16)],
        compiler_params=pltpu.CompilerParams(
            dimension_semantics=("arbitrary",), vmem_limit_bytes=VMEM_LIMIT),
        name="moe",
    )(*args)


def _sb_queries(u, sqg_ref, q_ref):
    sqg = sqg_ref[...] * (SB_HEAD_DIM ** -0.5 * LOG2_E)
    for hd in range(SB_HEADS):
        cols = slice(hd * SB_HEAD_DIM, (hd + 1) * SB_HEAD_DIM)
        q_ref[:, cols] = _rms(u[:, cols], sqg).astype(BF16)


def _mixer_b_pre_kernel(*refs, with_kv):
    if with_kv:
        (x_ref, g_ref, win_ref, sqg_ref, qg_ref, mk_ref, mv_ref, ein_ref, edn_ref,
         kvg_ref, wkv_ref, kg_ref, q_ref, mo_ref, ein_bf_ref, edn_bf_ref, k_ref, v_ref) = refs
    else:
        (x_ref, g_ref, win_ref, sqg_ref, qg_ref, mk_ref, mv_ref, ein_ref, edn_ref,
         q_ref, mo_ref, ein_bf_ref, edn_bf_ref) = refs
    ein_bf_ref[...] = ein_ref[...].astype(BF16)
    edn_bf_ref[...] = edn_ref[...].astype(BF16)
    x = x_ref[...]
    unit = x * lax.rsqrt(jnp.mean(x * x, axis=-1, keepdims=True) + EPS)
    u = _dot((unit * g_ref[...]).astype(BF16), win_ref[...])
    _sb_queries(u, sqg_ref, q_ref)
    mo_ref[...] = _mem_attention(u[:, SB_WIDTH:], qg_ref[...], mk_ref[...], mv_ref[...]).astype(BF16)
    if with_kv:
        kv = _dot((unit * kvg_ref[...]).astype(BF16), wkv_ref[...])
        for h in range(SB_HEADS):
            cols = slice(h * SB_HEAD_DIM, (h + 1) * SB_HEAD_DIM)
            k_ref[:, cols] = _rms(kv[:, cols], kg_ref[...]).astype(BF16)
        v_ref[...] = kv[:, SB_WIDTH:].astype(BF16)


def _mixer_b_pre(x, g, w_in, sqg, qg, mk_big, mv_big, moe_in, moe_down, j, layer, tm, kv=None):
    t = x.shape[0]
    full = _full_spec
    tokens = lambda width: pl.BlockSpec((tm, width), lambda i: (i, 0))
    ein_in, ein_out, ein_shape = _bf16_copy_specs(moe_in, layer, t // tm)
    edn_in, edn_out, edn_shape = _bf16_copy_specs(moe_down, layer, t // tm)
    args = [x, g, w_in, sqg, qg, mk_big, mv_big, moe_in, moe_down]
    in_specs = [tokens(D_MODEL), full((1, D_MODEL)), _layer_spec(w_in, j), full((1, SB_HEAD_DIM)),
                full((1, MEM_WIDTH)), _layer_spec(mk_big, layer), _layer_spec(mv_big, layer),
                ein_in, edn_in]
    out_specs = [tokens(SB_WIDTH), tokens(MEM_WIDTH), ein_out, edn_out]
    out_shape = [jax.ShapeDtypeStruct((t, SB_WIDTH), BF16), jax.ShapeDtypeStruct((t, MEM_WIDTH), BF16),
                 ein_shape, edn_shape]
    if kv is not None:
        kvg, w_kv, kg = kv
        args += [kvg, w_kv, kg]
        in_specs += [full((1, D_MODEL)), full(w_kv.shape), full((1, SB_HEAD_DIM))]
        out_specs += [tokens(SB_WIDTH)] * 2
        out_shape += [jax.ShapeDtypeStruct((t, SB_WIDTH), BF16)] * 2
    return pl.pallas_call(
        functools.partial(_mixer_b_pre_kernel, with_kv=kv is not None),
        grid=(t // tm,),
        in_specs=in_specs,
        out_specs=out_specs,
        out_shape=out_shape,
        compiler_params=pltpu.CompilerParams(
            dimension_semantics=("arbitrary",), vmem_limit_bytes=VMEM_LIMIT),
        name="mixer_b_pre",
    )(*args)


def _sb_kernel(q_ref, k_ref, v_ref, o_ref):
    first_block = pl.program_id(1) * SB_GROUP
    row = lax.broadcasted_iota(jnp.int32, (SB_BLOCK, SB_BLOCK), 0)
    col = lax.broadcasted_iota(jnp.int32, (SB_BLOCK, SB_BLOCK), 1)
    causal = col < row
    suffix_ones = (row >= col).astype(BF16)

    def log2_one_minus_sigmoid(z):
        return jnp.minimum(-z, 0.0) - jnp.log(1.0 + jnp.exp2(-jnp.abs(z))) * LOG2_E

    def earlier_key_block(q, kb, carry, acc):
        start = pl.multiple_of(kb * SB_BLOCK, SB_BLOCK)
        k = k_ref[pl.ds(start, SB_BLOCK), :]
        v = v_ref[pl.ds(start, SB_BLOCK), :]
        z = lax.dot_general(q, k, (((1,), (1,)), ((), ())), preferred_element_type=F32)
        within = _dot(log2_one_minus_sigmoid(z).astype(BF16), suffix_ones)
        w = jnp.exp2(z + within + carry)
        acc = acc + _dot(w.astype(BF16), v)
        return carry + within[:, 0:1], acc

    def block_rows(g, n=1):
        return slice(g * SB_BLOCK, (g + n) * SB_BLOCK)

    def run(first_step):
        offsets = range(0 if first_step else 1 - SB_UNROLLED_BLOCKS, SB_GROUP)
        users = {off: range(max(off, 0), min(off + SB_UNROLLED_BLOCKS, SB_GROUP)) for off in offsets}
        tiles = [(g, g - off) for off in offsets for g in users[off]]

        z = {}
        for off in offsets:
            start = pl.multiple_of((first_block + off) * SB_BLOCK, SB_BLOCK)
            k = k_ref[pl.ds(start, SB_BLOCK), :]
            gs = users[off]
            zz = lax.dot_general(q_ref[block_rows(gs[0], len(gs)), :], k, (((1,), (1,)), ((), ())),
                                 preferred_element_type=F32)
            for n, g in enumerate(gs):
                z[g, g - off] = zz[block_rows(n), :]

        parts = []
        for g, back in tiles:
            lf = log2_one_minus_sigmoid(z[g, back])
            if back == 0:
                lf = jnp.where(causal, lf, 0.0)
            parts.append(lf.astype(BF16))
        within_all = _dot(jnp.concatenate(parts, axis=0), suffix_ones)
        within = {t: within_all[block_rows(n), :] for n, t in enumerate(tiles)}

        carries, w = [], {}
        for g in range(SB_GROUP):
            carry = jnp.zeros((SB_BLOCK, 1), F32)
            for back in range(SB_UNROLLED_BLOCKS):
                if (g, back) not in within:
                    continue
                wt = jnp.exp2(z[g, back] + within[g, back] + carry)
                if back == 0:
                    wt = jnp.where(causal, wt, 0.0)
                w[g, back] = wt.astype(BF16)
                carry = carry + within[g, back][:, 0:1]
            carries.append(carry)

        accs = [jnp.zeros((SB_BLOCK, SB_HEAD_DIM), F32) for _ in range(SB_GROUP)]
        for off in offsets:
            start = pl.multiple_of((first_block + off) * SB_BLOCK, SB_BLOCK)
            v = v_ref[pl.ds(start, SB_BLOCK), :]
            gs = users[off]
            pv = _dot(jnp.concatenate([w[g, g - off] for g in gs], axis=0), v)
            for n, g in enumerate(gs):
                accs[g] = accs[g] + pv[block_rows(n), :]
        for g in range(SB_GROUP):
            o_ref[block_rows(g), :] = accs[g].astype(o_ref.dtype)

        worst = functools.reduce(jnp.maximum, carries)

        @pl.when(jnp.max(worst) > SB_ZERO_LOG2)
        def _():
            for g in range(SB_GROUP):
                q = q_ref[block_rows(g), :]

                def cond(state):
                    kb, carry, _ = state
                    return jnp.logical_and(kb >= 0, jnp.max(carry) > SB_ZERO_LOG2)

                def body(state):
                    kb, carry, acc = state
                    carry, acc = earlier_key_block(q, kb, carry, acc)
                    return kb - 1, carry, acc

                start = first_block + g - SB_UNROLLED_BLOCKS
                _, _, acc = lax.while_loop(cond, body, (start, carries[g], accs[g]))
                o_ref[block_rows(g), :] = acc.astype(o_ref.dtype)

    assert SB_GROUP >= SB_UNROLLED_BLOCKS - 1
    pl.when(first_block == 0)(lambda: run(True))
    pl.when(first_block > 0)(lambda: run(False))


def _sb_attention(q, k, v):
    t = q.shape[0]
    tq = SB_GROUP * SB_BLOCK
    return pl.pallas_call(
        _sb_kernel,
        grid=(SB_HEADS, t // tq),
        in_specs=[
            pl.BlockSpec((tq, SB_HEAD_DIM), lambda h, i: (i, h)),
            pl.BlockSpec((t, SB_HEAD_DIM), lambda h, i: (0, h)),
            pl.BlockSpec((t, SB_HEAD_DIM), lambda h, i: (0, h)),
        ],
        out_specs=pl.BlockSpec((tq, SB_HEAD_DIM), lambda h, i: (i, h)),
        out_shape=jax.ShapeDtypeStruct((t, SB_WIDTH), BF16),
        compiler_params=pltpu.CompilerParams(
            dimension_semantics=("arbitrary", "arbitrary"), vmem_limit_bytes=VMEM_LIMIT),
        name="sb_attention",
    )(q, k, v)


def _token_tile(t, tm):
    while t % tm:
        tm //= 2
    return tm


def kernel(x, mem, mem_norm_g, mix_norm_g, ffn_norm_g, w_in_a, conv_w, conv_ln_g, conv_ln_b, w_out_a, w_in_b, sb_q_norm_g, w_out_b, kv_norm_g, w_kv, sb_k_norm_g, w_mem_kv, mem_q_norm_g, mem_k_norm_g, router_g_w, router_g_b, router_e_w, router_e_b, moe_w_in, moe_w_down):
    b, t, d = x.shape
    assert b == 1 and d == D_MODEL and t % (SB_GROUP * SB_BLOCK) == 0
    tm = _token_tile(t, TOKEN_TILE)
    tm_moe = _token_tile(t, MOE_TOKEN_TILE)
    xt = x.reshape(t, d)

    mk_big, mv_big = _memkv(mem.reshape(MEM_TOKENS, d), mem_norm_g, w_mem_kv, mem_k_norm_g)
    mem_qg = jnp.tile(mem_q_norm_g, (1, MEM_HEADS)).reshape(DEPTH, 1, MEM_WIDTH)

    pad = ROUTER_LANES - N_GROUPS - N_EXPERTS
    w_router = jnp.concatenate(
        [router_g_w, router_e_w.transpose(0, 2, 1, 3).reshape(DEPTH, d, N_EXPERTS),
         jnp.zeros((DEPTH, d, pad), F32)], axis=-1)
    b_router = jnp.concatenate(
        [router_g_b, router_e_b.reshape(DEPTH, N_EXPERTS), jnp.zeros((DEPTH, pad), F32)],
        axis=-1).reshape(DEPTH, 1, ROUTER_LANES)
    moe_in = moe_w_in.reshape(DEPTH, N_EXPERTS * d, 2 * EXPERT_HIDDEN)
    moe_down = moe_w_down.reshape(DEPTH, N_EXPERTS * EXPERT_HIDDEN, d)

    w_in_a, w_out_a, w_in_b, w_out_b = (w.astype(BF16) for w in (w_in_a, w_out_a, w_in_b, w_out_b))
    shared_kv = (kv_norm_g.reshape(1, d), w_kv.astype(BF16), sb_k_norm_g.reshape(1, SB_HEAD_DIM))

    sb_k = sb_v = None
    for l in range(DEPTH):
        g_mix = mix_norm_g[l].reshape(1, d)
        out_proj = None
        if l < N_A_LAYERS:
            xt, e_in, e_down = _mixer_a(xt, g_mix, w_in_a, conv_w,
                                        conv_ln_g[l].reshape(1, CONV_CH), conv_ln_b[l].reshape(1, CONV_CH),
                                        mem_qg[l], mk_big, mv_big, w_out_a, moe_in, moe_down, l, tm)
        else:
            j = l - N_A_LAYERS
            outs = _mixer_b_pre(xt, g_mix, w_in_b, sb_q_norm_g[j].reshape(1, SB_HEAD_DIM),
                                mem_qg[l], mk_big, mv_big, moe_in, moe_down, j, l, tm,
                                kv=shared_kv if j == 0 else None)
            q, mo, e_in, e_down = outs[:4]
            if j == 0:
                sb_k, sb_v = outs[4:]
            out_proj = (_sb_attention(q, sb_k, sb_v), mo, w_out_b, j)
        xt = _moe(xt, ffn_norm_g[l].reshape(1, d), w_router, b_router,
                  e_in.reshape(N_EXPERTS, d, 2 * EXPERT_HIDDEN), e_down, l, tm_moe, out_proj=out_proj)
    return xt.reshape(b, t, d)
```

```python
import functools

import jax
import jax.numpy as jnp
from jax import lax
from jax.experimental import pallas as pl
from jax.experimental.pallas import tpu as pltpu

F32 = jnp.float32
BF16 = jnp.bfloat16

D_MODEL = 1024
DEPTH = 4
N_A_LAYERS = DEPTH // 2
MEM_TOKENS = 256
MEM_HEADS = 4
MEM_HEAD_DIM = 64
MEM_WIDTH = MEM_HEADS * MEM_HEAD_DIM
CONV_CH = D_MODEL - MEM_WIDTH
CONV_WIDTH = 31
SB_HEADS = 4
SB_HEAD_DIM = 128
SB_WIDTH = SB_HEADS * SB_HEAD_DIM
N_GROUPS = 4
EXPERTS_PER_GROUP = 4
N_EXPERTS = N_GROUPS * EXPERTS_PER_GROUP
EXPERT_HIDDEN = D_MODEL // 8
EPS = 1e-6

LANES = 128
SUBLANES = 8
CONV_HALO = 32
CONV_ROWS = 128
CONV_PAD = 2 * SUBLANES
SB_BLOCK = 128
SB_GROUP = 16
SB_UNROLLED_BLOCKS = 3
LOG2_E = 1.4426950408889634
SB_ZERO_LOG2 = -106.0 * LOG2_E
VMEM_LIMIT = 56 * 1024 * 1024
TOKEN_TILE = 512
MOE_TOKEN_TILE = 1024


def _dot(a, b):
    return jnp.dot(a, b, preferred_element_type=F32)


def _split(a):
    hi = a.astype(BF16)
    lo = (a - hi.astype(F32)).astype(BF16)
    return hi, lo


def _dot_exact_rhs(a, b):
    hi, lo = _split(a)
    return _dot(hi, b) + _dot(lo, b)


def _dot3(a, b):
    ah, al = _split(a)
    bh, bl = _split(b)
    n = b.shape[1]
    both = _dot(ah, jnp.concatenate([bh, bl], axis=1))
    return both[:, :n] + (both[:, n:] + _dot(al, bh))


def _rms(x, g):
    ms = jnp.mean(x * x, axis=-1, keepdims=True)
    return x * lax.rsqrt(ms + EPS) * g


def _sigmoid(x):
    return 0.5 * jnp.tanh(0.5 * x) + 0.5


def _full_spec(shape):
    return pl.BlockSpec(shape, lambda *_: (0,) * len(shape), pipeline_mode=pl.Buffered(1))


def _layer_spec(stacked, layer):
    rest = stacked.shape[1:]
    return pl.BlockSpec((None,) + rest, lambda *_: (layer,) + (0,) * len(rest),
                        pipeline_mode=pl.Buffered(1))


def _bf16_copy_specs(stacked, layer, n):
    rows, cols = stacked.shape[1:]
    step_rows = rows // n
    assert step_rows * n == rows
    block = lambda i: jnp.minimum(i, n - 1)
    return (pl.BlockSpec((None, step_rows, cols), lambda i: (layer, block(i), 0)),
            pl.BlockSpec((step_rows, cols), lambda i: (block(i), 0)),
            jax.ShapeDtypeStruct((rows, cols), BF16))


def _head_block_ones(n, head_dim):
    r = lax.broadcasted_iota(jnp.int32, (n, n), 0) // head_dim
    c = lax.broadcasted_iota(jnp.int32, (n, n), 1) // head_dim
    return (r == c).astype(BF16)


def _memkv_kernel(mem_ref, g_ref, w_ref, kg_ref, mk_ref, mv_ref):
    mem_n = _rms(mem_ref[...], g_ref[...]).astype(BF16)
    kv = _dot(mem_n, w_ref[...])
    mk = kv[:, :MEM_WIDTH]
    mv = kv[:, MEM_WIDTH:]
    ss = _dot_exact_rhs(mk * mk, _head_block_ones(MEM_WIDTH, MEM_HEAD_DIM))
    mk = mk * lax.rsqrt(ss * (1.0 / MEM_HEAD_DIM) + EPS) * (kg_ref[...] * MEM_HEAD_DIM ** -0.5)
    mk_t = mk.T
    mk_ref[...] = jnp.zeros(mk_ref.shape, mk_ref.dtype)
    lane_head = lax.broadcasted_iota(jnp.int32, mv.shape, 1) // MEM_HEAD_DIM
    for h in range(MEM_HEADS):
        rows = slice(h * MEM_HEAD_DIM, (h + 1) * MEM_HEAD_DIM)
        mk_ref[rows, h * MEM_TOKENS:(h + 1) * MEM_TOKENS] = mk_t[rows, :].astype(BF16)
        mv_ref[h * MEM_TOKENS:(h + 1) * MEM_TOKENS, :] = jnp.where(lane_head == h, mv, 0.0).astype(BF16)


def _memkv(mem, mem_norm_g, w_mem_kv, mem_k_norm_g):
    kg = jnp.tile(mem_k_norm_g, (1, MEM_HEADS)).reshape(DEPTH, 1, MEM_WIDTH)
    return pl.pallas_call(
        _memkv_kernel,
        grid=(DEPTH,),
        in_specs=[
            pl.BlockSpec((MEM_TOKENS, D_MODEL), lambda l: (0, 0)),
            pl.BlockSpec((1, D_MODEL), lambda l: (0, 0)),
            pl.BlockSpec((None, D_MODEL, 2 * MEM_WIDTH), lambda l: (l, 0, 0)),
            pl.BlockSpec((None, 1, MEM_WIDTH), lambda l: (l, 0, 0)),
        ],
        out_specs=[
            pl.BlockSpec((None, MEM_WIDTH, MEM_HEADS * MEM_TOKENS), lambda l: (l, 0, 0)),
            pl.BlockSpec((None, MEM_HEADS * MEM_TOKENS, MEM_WIDTH), lambda l: (l, 0, 0)),
        ],
        out_shape=[
            jax.ShapeDtypeStruct((DEPTH, MEM_WIDTH, MEM_HEADS * MEM_TOKENS), BF16),
            jax.ShapeDtypeStruct((DEPTH, MEM_HEADS * MEM_TOKENS, MEM_WIDTH), BF16),
        ],
        name="memkv",
    )(mem, mem_norm_g.reshape(1, D_MODEL), w_mem_kv.astype(BF16), kg)


def _mem_attention(q, qg, mk_big, mv_big):
    ss = _dot_exact_rhs(q * q, _head_block_ones(MEM_WIDTH, MEM_HEAD_DIM))
    qn = q * lax.rsqrt(ss * (1.0 / MEM_HEAD_DIM) + EPS) * qg
    s = _dot(qn.astype(BF16), mk_big)
    probs = []
    for h in range(MEM_HEADS):
        sh = s[:, h * MEM_TOKENS:(h + 1) * MEM_TOKENS]
        e = jnp.exp(sh - jnp.max(sh, axis=-1, keepdims=True))
        probs.append((e * (1.0 / jnp.sum(e, axis=-1, keepdims=True))).astype(BF16))
    return _dot(jnp.concatenate(probs, axis=1), mv_big)


def _mixer_a_kernel(*refs, n_side):
    x_ref, g_ref, win_ref, cw_ref, lng_ref, lnb_ref, qg_ref, mk_ref, mv_ref, wout_ref = refs[:10]
    side_in, o_ref, side_out = refs[10:10 + n_side], refs[10 + n_side], refs[11 + n_side:11 + 2 * n_side]
    cbuf_ref, conv_ref, xprev_ref, moprev_ref = refs[11 + 2 * n_side:]
    tm = x_ref.shape[0]
    for src, dst in zip(side_in, side_out):
        dst[...] = src[...].astype(BF16)

    @pl.when(pl.program_id(0) == 0)
    def _():
        for ref in (cbuf_ref, xprev_ref, moprev_ref):
            ref[...] = jnp.zeros(ref.shape, ref.dtype)

    x = x_ref[...]
    h = _rms(x, g_ref[...]).astype(BF16)
    u = _dot(h, win_ref[...])

    base = CONV_HALO - (CONV_WIDTH - 1)
    win = CONV_ROWS + SUBLANES
    for r0 in range(0, tm, CONV_ROWS):
        for c0 in range(0, CONV_CH, LANES):
            cols = slice(c0, c0 + LANES)
            acc = None
            for shift in range(SUBLANES):
                part = None
                for k in range(CONV_WIDTH):
                    if (base + k) % SUBLANES != shift:
                        continue
                    start = r0 + (base + k - shift)
                    term = cw_ref[k:k + 1, cols] * cbuf_ref[start:start + win, cols]
                    part = term if part is None else part + term
                part = part[shift:shift + CONV_ROWS, :]
                acc = part if acc is None else acc + part
            conv_ref[r0:r0 + CONV_ROWS, cols] = acc

    cbuf_ref[0:CONV_HALO, :] = cbuf_ref[tm:tm + CONV_HALO, :]
    cbuf_ref[CONV_HALO:CONV_HALO + tm, :] = u[:, :CONV_CH] * _sigmoid(u[:, CONV_CH:2 * CONV_CH])

    c = conv_ref[...]
    xc = c - jnp.mean(c, axis=-1, keepdims=True)
    var = jnp.mean(xc * xc, axis=-1, keepdims=True)
    y = xc * lax.rsqrt(var + EPS) * lng_ref[...] + lnb_ref[...]
    mixed = (y * _sigmoid(y)).astype(BF16)
    o_ref[...] = (xprev_ref[...] + _dot(mixed, wout_ref[0:CONV_CH, :])
                  + _dot(moprev_ref[...], wout_ref[CONV_CH:, :]))

    xprev_ref[...] = x
    moprev_ref[...] = _mem_attention(u[:, 2 * CONV_CH:], qg_ref[...], mk_ref[...],
                                     mv_ref[...]).astype(BF16)


def _mixer_a(x, g, w_in, conv_w, ln_g, ln_b, qg, mk_big, mv_big, w_out, side_casts, layer, tm):
    t = x.shape[0]
    n = t // tm
    full = _full_spec
    side = [_bf16_copy_specs(stacked, side_layer, n) for stacked, side_layer in side_casts]
    return pl.pallas_call(
        functools.partial(_mixer_a_kernel, n_side=len(side)),
        grid=(n + 1,),
        in_specs=[
            pl.BlockSpec((tm, D_MODEL), lambda i: (jnp.minimum(i, n - 1), 0)),
            full((1, D_MODEL)),
            full(w_in.shape),
            _layer_spec(conv_w, layer),
            full((1, CONV_CH)),
            full((1, CONV_CH)),
            full((1, MEM_WIDTH)),
            _layer_spec(mk_big, layer),
            _layer_spec(mv_big, layer),
            full(w_out.shape),
            *[s[0] for s in side],
        ],
        out_specs=[pl.BlockSpec((tm, D_MODEL), lambda i: (jnp.maximum(i - 1, 0), 0)),
                   *[s[1] for s in side]],
        out_shape=[jax.ShapeDtypeStruct((t, D_MODEL), F32), *[s[2] for s in side]],
        scratch_shapes=[
            pltpu.VMEM((tm + CONV_HALO + CONV_PAD, CONV_CH), F32),
            pltpu.VMEM((tm, CONV_CH), F32),
            pltpu.VMEM((tm, D_MODEL), F32),
            pltpu.VMEM((tm, MEM_WIDTH), BF16),
        ],
        compiler_params=pltpu.CompilerParams(
            dimension_semantics=("arbitrary",), vmem_limit_bytes=VMEM_LIMIT),
        name="mixer_a",
    )(x, g, w_in, conv_w, ln_g, ln_b, qg, mk_big, mv_big, w_out, *[w for w, _ in side_casts])


ROUTER_LANES = LANES
EXPERT_LANE0 = N_GROUPS


def _route(logits):
    neg = jnp.float32(-jnp.inf)
    lane = lax.broadcasted_iota(jnp.int32, logits.shape, 1)
    lane_f = lane.astype(F32)
    is_g = lane < N_GROUPS
    gl = jnp.where(is_g, logits, neg)
    gmax = jnp.max(gl, axis=-1, keepdims=True)
    gidx = jnp.min(jnp.where(gl == gmax, lane_f, float(ROUTER_LANES)), axis=-1, keepdims=True)
    gsum = jnp.sum(jnp.where(is_g, jnp.exp(gl - gmax), 0.0), axis=-1, keepdims=True)
    g_gate = 1.0 / gsum
    lane_group = ((lane - EXPERT_LANE0) // EXPERTS_PER_GROUP).astype(F32)
    sel = (lane >= EXPERT_LANE0) & (lane < EXPERT_LANE0 + N_EXPERTS) & (lane_group == gidx)
    sl = jnp.where(sel, logits, neg)
    m1 = jnp.max(sl, axis=-1, keepdims=True)
    i1 = jnp.min(jnp.where(sl == m1, lane_f, float(ROUTER_LANES)), axis=-1, keepdims=True)
    sl2 = jnp.where(lane_f == i1, neg, sl)
    m2 = jnp.max(sl2, axis=-1, keepdims=True)
    i2 = jnp.min(jnp.where(sl2 == m2, lane_f, float(ROUTER_LANES)), axis=-1, keepdims=True)
    e2 = jnp.exp(m2 - m1)
    w1 = g_gate / (1.0 + e2)
    w2 = w1 * e2
    return jnp.where(lane_f == i1, w1, jnp.where(lane_f == i2, w2, 0.0))


def _moe_kernel(*refs, with_out_proj):
    if with_out_proj:
        x_ref, sb_ref, mo_ref, wout_ref, g_ref, wr_ref, br_ref, win_ref, wdn_ref, o_ref, act_ref = refs
        x = (x_ref[...] + _dot(sb_ref[...], wout_ref[0:SB_WIDTH, :])
             + _dot(mo_ref[...], wout_ref[SB_WIDTH:, :]))
    else:
        x_ref, g_ref, wr_ref, br_ref, win_ref, wdn_ref, o_ref, act_ref = refs
        x = x_ref[...]
    hf = _rms(x, g_ref[...])
    hb = hf.astype(BF16)
    gate = _route(_dot3(hf, wr_ref[...]) + br_ref[...])
    for e in range(N_EXPERTS):
        hu = _dot(hb, win_ref[e])
        a = hu[:, :EXPERT_HIDDEN]
        act = a * _sigmoid(a) * hu[:, EXPERT_HIDDEN:] * gate[:, EXPERT_LANE0 + e:EXPERT_LANE0 + e + 1]
        act_ref[:, e * EXPERT_HIDDEN:(e + 1) * EXPERT_HIDDEN] = act.astype(BF16)
    o_ref[...] = x + _dot(act_ref[...], wdn_ref[...])


def _moe(x, g, w_router, b_router, w_in, w_down, layer, tm, out_proj=None):
    t = x.shape[0]
    tokens = lambda width: pl.BlockSpec((tm, width), lambda i: (i, 0))
    args, in_specs = [x], [tokens(D_MODEL)]
    if out_proj is not None:
        sb, mo, w_out, j = out_proj
        args += [sb, mo, w_out]
        in_specs += [tokens(SB_WIDTH), tokens(MEM_WIDTH), _layer_spec(w_out, j)]
    args += [g, w_router, b_router, w_in, w_down]
    in_specs += [_full_spec((1, D_MODEL)), _layer_spec(w_router, layer), _layer_spec(b_router, layer),
                 _full_spec(w_in.shape), _full_spec(w_down.shape)]
    return pl.pallas_call(
        functools.partial(_moe_kernel, with_out_proj=out_proj is not None),
        grid=(t // tm,),
        in_specs=in_specs,
        out_specs=tokens(D_MODEL),
        out_shape=jax.ShapeDtypeStruct((t, D_MODEL), F32),
        scratch_shapes=[pltpu.VMEM((tm, N_EXPERTS * EXPERT_HIDDEN), BF16)],
        compiler_params=pltpu.CompilerParams(
            dimension_semantics=("arbitrary",), vmem_limit_bytes=VMEM_LIMIT),
        name="moe",
    )(*args)


def _sb_queries(u, sqg_ref, q_ref):
    sqg = sqg_ref[...] * (SB_HEAD_DIM ** -0.5 * LOG2_E)
    for hd in range(SB_HEADS):
        cols = slice(hd * SB_HEAD_DIM, (hd + 1) * SB_HEAD_DIM)
        q_ref[:, cols] = _rms(u[:, cols], sqg).astype(BF16)


def _mixer_b_pre_kernel(*refs, with_kv):
    if with_kv:
        (x_ref, g_ref, win_ref, sqg_ref, qg_ref, mk_ref, mv_ref, ein_ref, edn_ref,
         kvg_ref, wkv_ref, kg_ref, q_ref, mo_ref, ein_bf_ref, edn_bf_ref, k_ref, v_ref) = refs
    else:
        (x_ref, g_ref, win_ref, sqg_ref, qg_ref, mk_ref, mv_ref, ein_ref, edn_ref,
         q_ref, mo_ref, ein_bf_ref, edn_bf_ref) = refs
    ein_bf_ref[...] = ein_ref[...].astype(BF16)
    edn_bf_ref[...] = edn_ref[...].astype(BF16)
    x = x_ref[...]
    unit = x * lax.rsqrt(jnp.mean(x * x, axis=-1, keepdims=True) + EPS)
    u = _dot((unit * g_ref[...]).astype(BF16), win_ref[...])
    _sb_queries(u, sqg_ref, q_ref)
    mo_ref[...] = _mem_attention(u[:, SB_WIDTH:], qg_ref[...], mk_ref[...], mv_ref[...]).astype(BF16)
    if with_kv:
        kv = _dot((unit * kvg_ref[...]).astype(BF16), wkv_ref[...])
        for h in range(SB_HEADS):
            cols = slice(h * SB_HEAD_DIM, (h + 1) * SB_HEAD_DIM)
            k_ref[:, cols] = _rms(kv[:, cols], kg_ref[...]).astype(BF16)
        v_ref[...] = kv[:, SB_WIDTH:].astype(BF16)


def _mixer_b_pre(x, g, w_in, sqg, qg, mk_big, mv_big, moe_in, moe_down, j, layer, tm, kv=None):
    t = x.shape[0]
    full = _full_spec
    tokens = lambda width: pl.BlockSpec((tm, width), lambda i: (i, 0))
    ein_in, ein_out, ein_shape = _bf16_copy_specs(moe_in, layer, t // tm)
    edn_in, edn_out, edn_shape = _bf16_copy_specs(moe_down, layer, t // tm)
    args = [x, g, w_in, sqg, qg, mk_big, mv_big, moe_in, moe_down]
    in_specs = [tokens(D_MODEL), full((1, D_MODEL)), _layer_spec(w_in, j), full((1, SB_HEAD_DIM)),
                full((1, MEM_WIDTH)), _layer_spec(mk_big, layer), _layer_spec(mv_big, layer),
                ein_in, edn_in]
    out_specs = [tokens(SB_WIDTH), tokens(MEM_WIDTH), ein_out, edn_out]
    out_shape = [jax.ShapeDtypeStruct((t, SB_WIDTH), BF16), jax.ShapeDtypeStruct((t, MEM_WIDTH), BF16),
                 ein_shape, edn_shape]
    if kv is not None:
        kvg, w_kv, kg = kv
        args += [kvg, w_kv, kg]
        in_specs += [full((1, D_MODEL)), full(w_kv.shape), full((1, SB_HEAD_DIM))]
        out_specs += [tokens(SB_WIDTH)] * 2
        out_shape += [jax.ShapeDtypeStruct((t, SB_WIDTH), BF16)] * 2
    return pl.pallas_call(
        functools.partial(_mixer_b_pre_kernel, with_kv=kv is not None),
        grid=(t // tm,),
        in_specs=in_specs,
        out_specs=out_specs,
        out_shape=out_shape,
        compiler_params=pltpu.CompilerParams(
            dimension_semantics=("arbitrary",), vmem_limit_bytes=VMEM_LIMIT),
        name="mixer_b_pre",
    )(*args)


def _sb_kernel(q_ref, k_ref, v_ref, o_ref):
    first_block = pl.program_id(1) * SB_GROUP
    row = lax.broadcasted_iota(jnp.int32, (SB_BLOCK, SB_BLOCK), 0)
    col = lax.broadcasted_iota(jnp.int32, (SB_BLOCK, SB_BLOCK), 1)
    causal = col < row
    suffix_ones = (row >= col).astype(BF16)

    def log2_one_minus_sigmoid(z):
        return jnp.minimum(-z, 0.0) - jnp.log(1.0 + jnp.exp2(-jnp.abs(z))) * LOG2_E

    def earlier_key_block(q, kb, carry, acc):
        start = pl.multiple_of(kb * SB_BLOCK, SB_BLOCK)
        k = k_ref[pl.ds(start, SB_BLOCK), :]
        v = v_ref[pl.ds(start, SB_BLOCK), :]
        z = lax.dot_general(q, k, (((1,), (1,)), ((), ())), preferred_element_type=F32)
        within = _dot(log2_one_minus_sigmoid(z).astype(BF16), suffix_ones)
        w = jnp.exp2(z + within + carry)
        acc = acc + _dot(w.astype(BF16), v)
        return carry + within[:, 0:1], acc

    def block_rows(g, n=1):
        return slice(g * SB_BLOCK, (g + n) * SB_BLOCK)

    def run(first_step):
        offsets = range(0 if first_step else 1 - SB_UNROLLED_BLOCKS, SB_GROUP)
        users = {off: range(max(off, 0), min(off + SB_UNROLLED_BLOCKS, SB_GROUP)) for off in offsets}
        tiles = [(g, g - off) for off in offsets for g in users[off]]

        z = {}
        for off in offsets:
            start = pl.multiple_of((first_block + off) * SB_BLOCK, SB_BLOCK)
            k = k_ref[pl.ds(start, SB_BLOCK), :]
            gs = users[off]
            zz = lax.dot_general(q_ref[block_rows(gs[0], len(gs)), :], k, (((1,), (1,)), ((), ())),
                                 preferred_element_type=F32)
            for n, g in enumerate(gs):
                z[g, g - off] = zz[block_rows(n), :]

        parts = []
        for g, back in tiles:
            lf = log2_one_minus_sigmoid(z[g, back])
            if back == 0:
                lf = jnp.where(causal, lf, 0.0)
            parts.append(lf.astype(BF16))
        within_all = _dot(jnp.concatenate(parts, axis=0), suffix_ones)
        within = {t: within_all[block_rows(n), :] for n, t in enumerate(tiles)}

        carries, w = [], {}
        for g in range(SB_GROUP):
            carry = jnp.zeros((SB_BLOCK, 1), F32)
            for back in range(SB_UNROLLED_BLOCKS):
                if (g, back) not in within:
                    continue
                wt = jnp.exp2(z[g, back] + within[g, back] + carry)
                if back == 0:
                    wt = jnp.where(causal, wt, 0.0)
                w[g, back] = wt.astype(BF16)
                carry = carry + within[g, back][:, 0:1]
            carries.append(carry)

        accs = [jnp.zeros((SB_BLOCK, SB_HEAD_DIM), F32) for _ in range(SB_GROUP)]
        for off in offsets:
            start = pl.multiple_of((first_block + off) * SB_BLOCK, SB_BLOCK)
            v = v_ref[pl.ds(start, SB_BLOCK), :]
            gs = users[off]
            pv = _dot(jnp.concatenate([w[g, g - off] for g in gs], axis=0), v)
            for n, g in enumerate(gs):
                accs[g] = accs[g] + pv[block_rows(n), :]
        for g in range(SB_GROUP):
            o_ref[block_rows(g), :] = accs[g].astype(o_ref.dtype)

        worst = functools.reduce(jnp.maximum, carries)

        @pl.when(jnp.max(worst) > SB_ZERO_LOG2)
        def _():
            for g in range(SB_GROUP):
                q = q_ref[block_rows(g), :]

                def cond(state):
                    kb, carry, _ = state
                    return jnp.logical_and(kb >= 0, jnp.max(carry) > SB_ZERO_LOG2)

                def body(state):
                    kb, carry, acc = state
                    carry, acc = earlier_key_block(q, kb, carry, acc)
                    return kb - 1, carry, acc

                start = first_block + g - SB_UNROLLED_BLOCKS
                _, _, acc = lax.while_loop(cond, body, (start, carries[g], accs[g]))
                o_ref[block_rows(g), :] = acc.astype(o_ref.dtype)

    assert SB_GROUP >= SB_UNROLLED_BLOCKS - 1
    pl.when(first_block == 0)(lambda: run(True))
    pl.when(first_block > 0)(lambda: run(False))


def _sb_attention(q, k, v):
    t = q.shape[0]
    tq = SB_GROUP * SB_BLOCK
    return pl.pallas_call(
        _sb_kernel,
        grid=(SB_HEADS, t // tq),
        in_specs=[
            pl.BlockSpec((tq, SB_HEAD_DIM), lambda h, i: (i, h)),
            pl.BlockSpec((t, SB_HEAD_DIM), lambda h, i: (0, h)),
            pl.BlockSpec((t, SB_HEAD_DIM), lambda h, i: (0, h)),
        ],
        out_specs=pl.BlockSpec((tq, SB_HEAD_DIM), lambda h, i: (i, h)),
        out_shape=jax.ShapeDtypeStruct((t, SB_WIDTH), BF16),
        compiler_params=pltpu.CompilerParams(
            dimension_semantics=("arbitrary", "arbitrary"), vmem_limit_bytes=VMEM_LIMIT),
        name="sb_attention",
    )(q, k, v)


def _token_tile(t, tm):
    while t % tm:
        tm //= 2
    return tm


def kernel(x, mem, mem_norm_g, mix_norm_g, ffn_norm_g, w_in_a, conv_w, conv_ln_g, conv_ln_b, w_out_a, w_in_b, sb_q_norm_g, w_out_b, kv_norm_g, w_kv, sb_k_norm_g, w_mem_kv, mem_q_norm_g, mem_k_norm_g, router_g_w, router_g_b, router_e_w, router_e_b, moe_w_in, moe_w_down):
    b, t, d = x.shape
    assert b == 1 and d == D_MODEL and t % (SB_GROUP * SB_BLOCK) == 0
    tm = _token_tile(t, TOKEN_TILE)
    tm_moe = _token_tile(t, MOE_TOKEN_TILE)
    xt = x.reshape(t, d)

    mk_big, mv_big = _memkv(mem.reshape(MEM_TOKENS, d), mem_norm_g, w_mem_kv, mem_k_norm_g)
    mem_qg = jnp.tile(mem_q_norm_g, (1, MEM_HEADS)).reshape(DEPTH, 1, MEM_WIDTH)

    pad = ROUTER_LANES - N_GROUPS - N_EXPERTS
    w_router = jnp.concatenate(
        [router_g_w, router_e_w.transpose(0, 2, 1, 3).reshape(DEPTH, d, N_EXPERTS),
         jnp.zeros((DEPTH, d, pad), F32)], axis=-1)
    b_router = jnp.concatenate(
        [router_g_b, router_e_b.reshape(DEPTH, N_EXPERTS), jnp.zeros((DEPTH, pad), F32)],
        axis=-1).reshape(DEPTH, 1, ROUTER_LANES)
    moe_in = moe_w_in.reshape(DEPTH, N_EXPERTS * d, 2 * EXPERT_HIDDEN)
    moe_down = moe_w_down.reshape(DEPTH, N_EXPERTS * EXPERT_HIDDEN, d)

    mix_in = {0: w_in_a[0].astype(BF16)}
    mix_out = {0: w_out_a[0].astype(BF16)}
    later = [(w_in_a, l) for l in range(1, N_A_LAYERS)] + [(w_out_a, l) for l in range(1, N_A_LAYERS)]
    later += [(w.reshape(1, -1, w.shape[-1]), 0) for w in (w_in_b, w_out_b, w_kv)]

    sb_k = sb_v = None
    for l in range(DEPTH):
        g_mix = mix_norm_g[l].reshape(1, d)
        out_proj = None
        if l < N_A_LAYERS:
            side = [(moe_in, l), (moe_down, l)] + (later if l == 0 else [])
            xt, e_in, e_down, *copies = _mixer_a(
                xt, g_mix, mix_in[l], conv_w, conv_ln_g[l].reshape(1, CONV_CH),
                conv_ln_b[l].reshape(1, CONV_CH), mem_qg[l], mk_big, mv_big, mix_out[l], side, l, tm)
            if l == 0:
                for k, l_a in enumerate(range(1, N_A_LAYERS)):
                    mix_in[l_a] = copies[k]
                    mix_out[l_a] = copies[N_A_LAYERS - 1 + k]
                w_in_b16, w_out_b16, w_kv16 = copies[2 * (N_A_LAYERS - 1):]
                w_in_b16 = w_in_b16.reshape(w_in_b.shape)
                w_out_b16 = w_out_b16.reshape(w_out_b.shape)
                shared_kv = (kv_norm_g.reshape(1, d), w_kv16, sb_k_norm_g.reshape(1, SB_HEAD_DIM))
        else:
            j = l - N_A_LAYERS
            outs = _mixer_b_pre(xt, g_mix, w_in_b16, sb_q_norm_g[j].reshape(1, SB_HEAD_DIM),
                                mem_qg[l], mk_big, mv_big, moe_in, moe_down, j, l, tm,
                                kv=shared_kv if j == 0 else None)
            q, mo, e_in, e_down = outs[:4]
            if j == 0:
                sb_k, sb_v = outs[4:]
            out_proj = (_sb_attention(q, sb_k, sb_v), mo, w_out_b16, j)
        xt = _moe(xt, ffn_norm_g[l].reshape(1, d), w_router, b_router,
                  e_in.reshape(N_EXPERTS, d, 2 * EXPERT_HIDDEN), e_down, l, tm_moe, out_proj=out_proj)
    return xt.reshape(b, t, d)
```

```python
import functools

import jax
import jax.numpy as jnp
from jax import lax
from jax.experimental import pallas as pl
from jax.experimental.pallas import tpu as pltpu

F32 = jnp.float32
BF16 = jnp.bfloat16

D_MODEL = 1024
DEPTH = 4
N_A_LAYERS = DEPTH // 2
MEM_TOKENS = 256
MEM_HEADS = 4
MEM_HEAD_DIM = 64
MEM_WIDTH = MEM_HEADS * MEM_HEAD_DIM
CONV_CH = D_MODEL - MEM_WIDTH
CONV_WIDTH = 31
SB_HEADS = 4
SB_HEAD_DIM = 128
SB_WIDTH = SB_HEADS * SB_HEAD_DIM
N_GROUPS = 4
EXPERTS_PER_GROUP = 4
N_EXPERTS = N_GROUPS * EXPERTS_PER_GROUP
EXPERT_HIDDEN = D_MODEL // 8
EPS = 1e-6

LANES = 128
SUBLANES = 8
CONV_HALO = 32
CONV_ROWS = 128
CONV_PAD = 2 * SUBLANES
SB_BLOCK = 128
SB_GROUP = 16
SB_UNROLLED_BLOCKS = 3
LOG2_E = 1.4426950408889634
SB_ZERO_LOG2 = -106.0 * LOG2_E
VMEM_LIMIT = 56 * 1024 * 1024
CONV_TOKEN_TILE = 512
WIDE_TOKEN_TILE = 1024


def _dot(a, b):
    return jnp.dot(a, b, preferred_element_type=F32)


def _split(a):
    hi = a.astype(BF16)
    lo = (a - hi.astype(F32)).astype(BF16)
    return hi, lo


def _dot_exact_rhs(a, b):
    hi, lo = _split(a)
    return _dot(hi, b) + _dot(lo, b)


def _dot3(a, b):
    ah, al = _split(a)
    bh, bl = _split(b)
    n = b.shape[1]
    both = _dot(ah, jnp.concatenate([bh, bl], axis=1))
    return both[:, :n] + (both[:, n:] + _dot(al, bh))


def _rms(x, g):
    ms = jnp.mean(x * x, axis=-1, keepdims=True)
    return x * lax.rsqrt(ms + EPS) * g


def _sigmoid(x):
    return 0.5 * jnp.tanh(0.5 * x) + 0.5


def _silu(x):
    h = 0.5 * x
    return h * jnp.tanh(h) + h


def _full_spec(shape):
    return pl.BlockSpec(shape, lambda *_: (0,) * len(shape), pipeline_mode=pl.Buffered(1))


def _layer_spec(stacked, layer):
    rest = stacked.shape[1:]
    return pl.BlockSpec((None,) + rest, lambda *_: (layer,) + (0,) * len(rest),
                        pipeline_mode=pl.Buffered(1))


def _bf16_copy_specs(stacked, layer, n):
    rows, cols = stacked.shape[1:]
    step_rows = rows // n
    assert step_rows * n == rows
    block = lambda i: jnp.minimum(i, n - 1)
    return (pl.BlockSpec((None, step_rows, cols), lambda i: (layer, block(i), 0)),
            pl.BlockSpec((step_rows, cols), lambda i: (block(i), 0)),
            jax.ShapeDtypeStruct((rows, cols), BF16))


def _head_block_ones(n, head_dim):
    r = lax.broadcasted_iota(jnp.int32, (n, n), 0) // head_dim
    c = lax.broadcasted_iota(jnp.int32, (n, n), 1) // head_dim
    return (r == c).astype(BF16)


def _memkv_kernel(mem_ref, g_ref, w_ref, kg_ref, mk_ref, mv_ref):
    mem_n = _rms(mem_ref[...], g_ref[...]).astype(BF16)
    kv = _dot(mem_n, w_ref[...])
    mk = kv[:, :MEM_WIDTH]
    mv = kv[:, MEM_WIDTH:]
    ss = _dot_exact_rhs(mk * mk, _head_block_ones(MEM_WIDTH, MEM_HEAD_DIM))
    mk = mk * lax.rsqrt(ss * (1.0 / MEM_HEAD_DIM) + EPS) * (kg_ref[...] * MEM_HEAD_DIM ** -0.5)
    mk_t = mk.T
    mk_ref[...] = jnp.zeros(mk_ref.shape, mk_ref.dtype)
    lane_head = lax.broadcasted_iota(jnp.int32, mv.shape, 1) // MEM_HEAD_DIM
    for h in range(MEM_HEADS):
        rows = slice(h * MEM_HEAD_DIM, (h + 1) * MEM_HEAD_DIM)
        mk_ref[rows, h * MEM_TOKENS:(h + 1) * MEM_TOKENS] = mk_t[rows, :].astype(BF16)
        mv_ref[h * MEM_TOKENS:(h + 1) * MEM_TOKENS, :] = jnp.where(lane_head == h, mv, 0.0).astype(BF16)


def _memkv(mem, mem_norm_g, w_mem_kv, mem_k_norm_g):
    kg = jnp.tile(mem_k_norm_g, (1, MEM_HEADS)).reshape(DEPTH, 1, MEM_WIDTH)
    return pl.pallas_call(
        _memkv_kernel,
        grid=(DEPTH,),
        in_specs=[
            pl.BlockSpec((MEM_TOKENS, D_MODEL), lambda l: (0, 0)),
            pl.BlockSpec((1, D_MODEL), lambda l: (0, 0)),
            pl.BlockSpec((None, D_MODEL, 2 * MEM_WIDTH), lambda l: (l, 0, 0)),
            pl.BlockSpec((None, 1, MEM_WIDTH), lambda l: (l, 0, 0)),
        ],
        out_specs=[
            pl.BlockSpec((None, MEM_WIDTH, MEM_HEADS * MEM_TOKENS), lambda l: (l, 0, 0)),
            pl.BlockSpec((None, MEM_HEADS * MEM_TOKENS, MEM_WIDTH), lambda l: (l, 0, 0)),
        ],
        out_shape=[
            jax.ShapeDtypeStruct((DEPTH, MEM_WIDTH, MEM_HEADS * MEM_TOKENS), BF16),
            jax.ShapeDtypeStruct((DEPTH, MEM_HEADS * MEM_TOKENS, MEM_WIDTH), BF16),
        ],
        name="memkv",
    )(mem, mem_norm_g.reshape(1, D_MODEL), w_mem_kv.astype(BF16), kg)


def _mem_attention(q, qg, mk_big, mv_big):
    ss = _dot_exact_rhs(q * q, _head_block_ones(MEM_WIDTH, MEM_HEAD_DIM))
    qn = q * lax.rsqrt(ss * (1.0 / MEM_HEAD_DIM) + EPS) * qg
    s = _dot(qn.astype(BF16), mk_big)
    probs = []
    for h in range(MEM_HEADS):
        sh = s[:, h * MEM_TOKENS:(h + 1) * MEM_TOKENS]
        e = jnp.exp(sh - jnp.max(sh, axis=-1, keepdims=True))
        probs.append((e * (1.0 / jnp.sum(e, axis=-1, keepdims=True))).astype(BF16))
    return _dot(jnp.concatenate(probs, axis=1), mv_big)


def _mixer_a_kernel(*refs, n_side):
    x_ref, g_ref, win_ref, cw_ref, lng_ref, lnb_ref, qg_ref, mk_ref, mv_ref, wout_ref = refs[:10]
    side_in, o_ref, side_out = refs[10:10 + n_side], refs[10 + n_side], refs[11 + n_side:11 + 2 * n_side]
    cbuf_ref, conv_ref, xprev_ref, moprev_ref = refs[11 + 2 * n_side:]
    tm = x_ref.shape[0]
    for src, dst in zip(side_in, side_out):
        dst[...] = src[...].astype(BF16)

    @pl.when(pl.program_id(0) == 0)
    def _():
        for ref in (cbuf_ref, xprev_ref, moprev_ref):
            ref[...] = jnp.zeros(ref.shape, ref.dtype)

    x = x_ref[...]
    h = _rms(x, g_ref[...]).astype(BF16)
    u = _dot(h, win_ref[...])

    base = CONV_HALO - (CONV_WIDTH - 1)
    win = CONV_ROWS + SUBLANES
    for r0 in range(0, tm, CONV_ROWS):
        for c0 in range(0, CONV_CH, LANES):
            cols = slice(c0, c0 + LANES)
            acc = None
            for shift in range(SUBLANES):
                part = None
                for k in range(CONV_WIDTH):
                    if (base + k) % SUBLANES != shift:
                        continue
                    start = r0 + (base + k - shift)
                    term = cw_ref[k:k + 1, cols] * cbuf_ref[start:start + win, cols]
                    part = term if part is None else part + term
                part = part[shift:shift + CONV_ROWS, :]
                acc = part if acc is None else acc + part
            conv_ref[r0:r0 + CONV_ROWS, cols] = acc

    cbuf_ref[0:CONV_HALO, :] = cbuf_ref[tm:tm + CONV_HALO, :]
    cbuf_ref[CONV_HALO:CONV_HALO + tm, :] = u[:, :CONV_CH] * _sigmoid(u[:, CONV_CH:2 * CONV_CH])

    c = conv_ref[...]
    xc = c - jnp.mean(c, axis=-1, keepdims=True)
    var = jnp.mean(xc * xc, axis=-1, keepdims=True)
    y = xc * lax.rsqrt(var + EPS) * lng_ref[...] + lnb_ref[...]
    mixed = _silu(y).astype(BF16)
    o_ref[...] = (xprev_ref[...] + _dot(mixed, wout_ref[0:CONV_CH, :])
                  + _dot(moprev_ref[...], wout_ref[CONV_CH:, :]))

    xprev_ref[...] = x
    moprev_ref[...] = _mem_attention(u[:, 2 * CONV_CH:], qg_ref[...], mk_ref[...],
                                     mv_ref[...]).astype(BF16)


def _mixer_a(x, g, w_in, conv_w, ln_g, ln_b, qg, mk_big, mv_big, w_out, side_casts, layer, tm):
    t = x.shape[0]
    n = t // tm
    full = _full_spec
    side = [_bf16_copy_specs(stacked, side_layer, n) for stacked, side_layer in side_casts]
    return pl.pallas_call(
        functools.partial(_mixer_a_kernel, n_side=len(side)),
        grid=(n + 1,),
        in_specs=[
            pl.BlockSpec((tm, D_MODEL), lambda i: (jnp.minimum(i, n - 1), 0)),
            full((1, D_MODEL)),
            full(w_in.shape),
            _layer_spec(conv_w, layer),
            full((1, CONV_CH)),
            full((1, CONV_CH)),
            full((1, MEM_WIDTH)),
            _layer_spec(mk_big, layer),
            _layer_spec(mv_big, layer),
            full(w_out.shape),
            *[s[0] for s in side],
        ],
        out_specs=[pl.BlockSpec((tm, D_MODEL), lambda i: (jnp.maximum(i - 1, 0), 0)),
                   *[s[1] for s in side]],
        out_shape=[jax.ShapeDtypeStruct((t, D_MODEL), F32), *[s[2] for s in side]],
        scratch_shapes=[
            pltpu.VMEM((tm + CONV_HALO + CONV_PAD, CONV_CH), F32),
            pltpu.VMEM((tm, CONV_CH), F32),
            pltpu.VMEM((tm, D_MODEL), F32),
            pltpu.VMEM((tm, MEM_WIDTH), BF16),
        ],
        compiler_params=pltpu.CompilerParams(
            dimension_semantics=("arbitrary",), vmem_limit_bytes=VMEM_LIMIT),
        name="mixer_a",
    )(x, g, w_in, conv_w, ln_g, ln_b, qg, mk_big, mv_big, w_out, *[w for w, _ in side_casts])


ROUTER_LANES = LANES
EXPERT_LANE0 = N_GROUPS


def _route(logits):
    neg = jnp.float32(-jnp.inf)
    lane = lax.broadcasted_iota(jnp.int32, logits.shape, 1)
    lane_f = lane.astype(F32)
    is_g = lane < N_GROUPS
    gl = jnp.where(is_g, logits, neg)
    gmax = jnp.max(gl, axis=-1, keepdims=True)
    gidx = jnp.min(jnp.where(gl == gmax, lane_f, float(ROUTER_LANES)), axis=-1, keepdims=True)
    gsum = jnp.sum(jnp.where(is_g, jnp.exp(gl - gmax), 0.0), axis=-1, keepdims=True)
    g_gate = 1.0 / gsum
    lane_group = ((lane - EXPERT_LANE0) // EXPERTS_PER_GROUP).astype(F32)
    sel = (lane >= EXPERT_LANE0) & (lane < EXPERT_LANE0 + N_EXPERTS) & (lane_group == gidx)
    sl = jnp.where(sel, logits, neg)
    m1 = jnp.max(sl, axis=-1, keepdims=True)
    i1 = jnp.min(jnp.where(sl == m1, lane_f, float(ROUTER_LANES)), axis=-1, keepdims=True)
    sl2 = jnp.where(lane_f == i1, neg, sl)
    m2 = jnp.max(sl2, axis=-1, keepdims=True)
    i2 = jnp.min(jnp.where(sl2 == m2, lane_f, float(ROUTER_LANES)), axis=-1, keepdims=True)
    e2 = jnp.exp(m2 - m1)
    w1 = g_gate / (1.0 + e2)
    w2 = w1 * e2
    return jnp.where(lane_f == i1, w1, jnp.where(lane_f == i2, w2, 0.0))


def _moe_kernel(*refs, with_out_proj):
    if with_out_proj:
        x_ref, sb_ref, mo_ref, wout_ref, g_ref, wr_ref, br_ref, win_ref, wdn_ref, o_ref, act_ref = refs
        x = (x_ref[...] + _dot(sb_ref[...], wout_ref[0:SB_WIDTH, :])
             + _dot(mo_ref[...], wout_ref[SB_WIDTH:, :]))
    else:
        x_ref, g_ref, wr_ref, br_ref, win_ref, wdn_ref, o_ref, act_ref = refs
        x = x_ref[...]
    hf = _rms(x, g_ref[...])
    hb = hf.astype(BF16)
    gate = _route(_dot3(hf, wr_ref[...]) + br_ref[...])
    for e in range(N_EXPERTS):
        hu = _dot(hb, win_ref[e])
        a = hu[:, :EXPERT_HIDDEN]
        act = _silu(a) * hu[:, EXPERT_HIDDEN:] * gate[:, EXPERT_LANE0 + e:EXPERT_LANE0 + e + 1]
        act_ref[:, e * EXPERT_HIDDEN:(e + 1) * EXPERT_HIDDEN] = act.astype(BF16)
    o_ref[...] = x + _dot(act_ref[...], wdn_ref[...])


def _moe(x, g, w_router, b_router, w_in, w_down, layer, tm, out_proj=None):
    t = x.shape[0]
    tokens = lambda width: pl.BlockSpec((tm, width), lambda i: (i, 0))
    args, in_specs = [x], [tokens(D_MODEL)]
    if out_proj is not None:
        sb, mo, w_out, j = out_proj
        args += [sb, mo, w_out]
        in_specs += [tokens(SB_WIDTH), tokens(MEM_WIDTH), _layer_spec(w_out, j)]
    args += [g, w_router, b_router, w_in, w_down]
    in_specs += [_full_spec((1, D_MODEL)), _layer_spec(w_router, layer), _layer_spec(b_router, layer),
                 _full_spec(w_in.shape), _full_spec(w_down.shape)]
    return pl.pallas_call(
        functools.partial(_moe_kernel, with_out_proj=out_proj is not None),
        grid=(t // tm,),
        in_specs=in_specs,
        out_specs=tokens(D_MODEL),
        out_shape=jax.ShapeDtypeStruct((t, D_MODEL), F32),
        scratch_shapes=[pltpu.VMEM((tm, N_EXPERTS * EXPERT_HIDDEN), BF16)],
        compiler_params=pltpu.CompilerParams(
            dimension_semantics=("arbitrary",), vmem_limit_bytes=VMEM_LIMIT),
        name="moe",
    )(*args)


def _sb_queries(u, sqg_ref, q_ref):
    sqg = sqg_ref[...] * (SB_HEAD_DIM ** -0.5 * LOG2_E)
    for hd in range(SB_HEADS):
        cols = slice(hd * SB_HEAD_DIM, (hd + 1) * SB_HEAD_DIM)
        q_ref[:, cols] = _rms(u[:, cols], sqg).astype(BF16)


def _mixer_b_pre_kernel(*refs, with_kv):
    if with_kv:
        (x_ref, g_ref, win_ref, sqg_ref, qg_ref, mk_ref, mv_ref, ein_ref, edn_ref,
         kvg_ref, wkv_ref, kg_ref, q_ref, mo_ref, ein_bf_ref, edn_bf_ref, k_ref, v_ref) = refs
    else:
        (x_ref, g_ref, win_ref, sqg_ref, qg_ref, mk_ref, mv_ref, ein_ref, edn_ref,
         q_ref, mo_ref, ein_bf_ref, edn_bf_ref) = refs
    ein_bf_ref[...] = ein_ref[...].astype(BF16)
    edn_bf_ref[...] = edn_ref[...].astype(BF16)
    x = x_ref[...]
    unit = x * lax.rsqrt(jnp.mean(x * x, axis=-1, keepdims=True) + EPS)
    u = _dot((unit * g_ref[...]).astype(BF16), win_ref[...])
    _sb_queries(u, sqg_ref, q_ref)
    mo_ref[...] = _mem_attention(u[:, SB_WIDTH:], qg_ref[...], mk_ref[...], mv_ref[...]).astype(BF16)
    if with_kv:
        kv = _dot((unit * kvg_ref[...]).astype(BF16), wkv_ref[...])
        for h in range(SB_HEADS):
            cols = slice(h * SB_HEAD_DIM, (h + 1) * SB_HEAD_DIM)
            k_ref[:, cols] = _rms(kv[:, cols], kg_ref[...]).astype(BF16)
        v_ref[...] = kv[:, SB_WIDTH:].astype(BF16)


def _mixer_b_pre(x, g, w_in, sqg, qg, mk_big, mv_big, moe_in, moe_down, j, layer, tm, kv=None):
    t = x.shape[0]
    full = _full_spec
    tokens = lambda width: pl.BlockSpec((tm, width), lambda i: (i, 0))
    ein_in, ein_out, ein_shape = _bf16_copy_specs(moe_in, layer, t // tm)
    edn_in, edn_out, edn_shape = _bf16_copy_specs(moe_down, layer, t // tm)
    args = [x, g, w_in, sqg, qg, mk_big, mv_big, moe_in, moe_down]
    in_specs = [tokens(D_MODEL), full((1, D_MODEL)), _layer_spec(w_in, j), full((1, SB_HEAD_DIM)),
                full((1, MEM_WIDTH)), _layer_spec(mk_big, layer), _layer_spec(mv_big, layer),
                ein_in, edn_in]
    out_specs = [tokens(SB_WIDTH), tokens(MEM_WIDTH), ein_out, edn_out]
    out_shape = [jax.ShapeDtypeStruct((t, SB_WIDTH), BF16), jax.ShapeDtypeStruct((t, MEM_WIDTH), BF16),
                 ein_shape, edn_shape]
    if kv is not None:
        kvg, w_kv, kg = kv
        args += [kvg, w_kv, kg]
        in_specs += [full((1, D_MODEL)), full(w_kv.shape), full((1, SB_HEAD_DIM))]
        out_specs += [tokens(SB_WIDTH)] * 2
        out_shape += [jax.ShapeDtypeStruct((t, SB_WIDTH), BF16)] * 2
    return pl.pallas_call(
        functools.partial(_mixer_b_pre_kernel, with_kv=kv is not None),
        grid=(t // tm,),
        in_specs=in_specs,
        out_specs=out_specs,
        out_shape=out_shape,
        compiler_params=pltpu.CompilerParams(
            dimension_semantics=("arbitrary",), vmem_limit_bytes=VMEM_LIMIT),
        name="mixer_b_pre",
    )(*args)


def _sb_kernel(q_ref, k_ref, v_ref, o_ref):
    first_block = pl.program_id(1) * SB_GROUP
    row = lax.broadcasted_iota(jnp.int32, (SB_BLOCK, SB_BLOCK), 0)
    col = lax.broadcasted_iota(jnp.int32, (SB_BLOCK, SB_BLOCK), 1)
    causal = col < row
    suffix_ones = (row >= col).astype(BF16)

    def log2_one_minus_sigmoid(z):
        return jnp.minimum(-z, 0.0) - jnp.log(1.0 + jnp.exp2(-jnp.abs(z))) * LOG2_E

    def earlier_key_block(q, kb, carry, acc):
        start = pl.multiple_of(kb * SB_BLOCK, SB_BLOCK)
        k = k_ref[pl.ds(start, SB_BLOCK), :]
        v = v_ref[pl.ds(start, SB_BLOCK), :]
        z = lax.dot_general(q, k, (((1,), (1,)), ((), ())), preferred_element_type=F32)
        within = _dot(log2_one_minus_sigmoid(z).astype(BF16), suffix_ones)
        w = jnp.exp2(z + within + carry)
        acc = acc + _dot(w.astype(BF16), v)
        return carry + within[:, 0:1], acc

    def block_rows(g, n=1):
        return slice(g * SB_BLOCK, (g + n) * SB_BLOCK)

    def run(first_step):
        offsets = range(0 if first_step else 1 - SB_UNROLLED_BLOCKS, SB_GROUP)
        users = {off: range(max(off, 0), min(off + SB_UNROLLED_BLOCKS, SB_GROUP)) for off in offsets}
        tiles = [(g, g - off) for off in offsets for g in users[off]]

        z = {}
        for off in offsets:
            start = pl.multiple_of((first_block + off) * SB_BLOCK, SB_BLOCK)
            k = k_ref[pl.ds(start, SB_BLOCK), :]
            gs = users[off]
            zz = lax.dot_general(q_ref[block_rows(gs[0], len(gs)), :], k, (((1,), (1,)), ((), ())),
                                 preferred_element_type=F32)
            for n, g in enumerate(gs):
                z[g, g - off] = zz[block_rows(n), :]

        parts = []
        for g, back in tiles:
            lf = log2_one_minus_sigmoid(z[g, back])
            if back == 0:
                lf = jnp.where(causal, lf, 0.0)
            parts.append(lf.astype(BF16))
        within_all = _dot(jnp.concatenate(parts, axis=0), suffix_ones)
        within = {t: within_all[block_rows(n), :] for n, t in enumerate(tiles)}

        carries, w = [], {}
        for g in range(SB_GROUP):
            carry = jnp.zeros((SB_BLOCK, 1), F32)
            for back in range(SB_UNROLLED_BLOCKS):
                if (g, back) not in within:
                    continue
                wt = jnp.exp2(z[g, back] + within[g, back] + carry)
                if back == 0:
                    wt = jnp.where(causal, wt, 0.0)
                w[g, back] = wt.astype(BF16)
                carry = carry + within[g, back][:, 0:1]
            carries.append(carry)

        accs = [jnp.zeros((SB_BLOCK, SB_HEAD_DIM), F32) for _ in range(SB_GROUP)]
        for off in offsets:
            start = pl.multiple_of((first_block + off) * SB_BLOCK, SB_BLOCK)
            v = v_ref[pl.ds(start, SB_BLOCK), :]
            gs = users[off]
            pv = _dot(jnp.concatenate([w[g, g - off] for g in gs], axis=0), v)
            for n, g in enumerate(gs):
                accs[g] = accs[g] + pv[block_rows(n), :]
        for g in range(SB_GROUP):
            o_ref[block_rows(g), :] = accs[g].astype(o_ref.dtype)

        worst = functools.reduce(jnp.maximum, carries)

        @pl.when(jnp.max(worst) > SB_ZERO_LOG2)
        def _():
            for g in range(SB_GROUP):
                q = q_ref[block_rows(g), :]

                def cond(state):
                    kb, carry, _ = state
                    return jnp.logical_and(kb >= 0, jnp.max(carry) > SB_ZERO_LOG2)

                def body(state):
                    kb, carry, acc = state
                    carry, acc = earlier_key_block(q, kb, carry, acc)
                    return kb - 1, carry, acc

                start = first_block + g - SB_UNROLLED_BLOCKS
                _, _, acc = lax.while_loop(cond, body, (start, carries[g], accs[g]))
                o_ref[block_rows(g), :] = acc.astype(o_ref.dtype)

    assert SB_GROUP >= SB_UNROLLED_BLOCKS - 1
    pl.when(first_block == 0)(lambda: run(True))
    pl.when(first_block > 0)(lambda: run(False))


def _sb_attention(q, k, v):
    t = q.shape[0]
    tq = SB_GROUP * SB_BLOCK
    return pl.pallas_call(
        _sb_kernel,
        grid=(SB_HEADS, t // tq),
        in_specs=[
            pl.BlockSpec((tq, SB_HEAD_DIM), lambda h, i: (i, h)),
            pl.BlockSpec((t, SB_HEAD_DIM), lambda h, i: (0, h)),
            pl.BlockSpec((t, SB_HEAD_DIM), lambda h, i: (0, h)),
        ],
        out_specs=pl.BlockSpec((tq, SB_HEAD_DIM), lambda h, i: (i, h)),
        out_shape=jax.ShapeDtypeStruct((t, SB_WIDTH), BF16),
        compiler_params=pltpu.CompilerParams(
            dimension_semantics=("arbitrary", "arbitrary"), vmem_limit_bytes=VMEM_LIMIT),
        name="sb_attention",
    )(q, k, v)


def _token_tile(t, tm):
    while t % tm:
        tm //= 2
    return tm


def kernel(x, mem, mem_norm_g, mix_norm_g, ffn_norm_g, w_in_a, conv_w, conv_ln_g, conv_ln_b, w_out_a, w_in_b, sb_q_norm_g, w_out_b, kv_norm_g, w_kv, sb_k_norm_g, w_mem_kv, mem_q_norm_g, mem_k_norm_g, router_g_w, router_g_b, router_e_w, router_e_b, moe_w_in, moe_w_down):
    b, t, d = x.shape
    assert b == 1 and d == D_MODEL and t % (SB_GROUP * SB_BLOCK) == 0
    tm = _token_tile(t, CONV_TOKEN_TILE)
    tm_wide = _token_tile(t, WIDE_TOKEN_TILE)
    xt = x.reshape(t, d)

    mk_big, mv_big = _memkv(mem.reshape(MEM_TOKENS, d), mem_norm_g, w_mem_kv, mem_k_norm_g)
    mem_qg = jnp.tile(mem_q_norm_g, (1, MEM_HEADS)).reshape(DEPTH, 1, MEM_WIDTH)

    pad = ROUTER_LANES - N_GROUPS - N_EXPERTS
    w_router = jnp.concatenate(
        [router_g_w, router_e_w.transpose(0, 2, 1, 3).reshape(DEPTH, d, N_EXPERTS),
         jnp.zeros((DEPTH, d, pad), F32)], axis=-1)
    b_router = jnp.concatenate(
        [router_g_b, router_e_b.reshape(DEPTH, N_EXPERTS), jnp.zeros((DEPTH, pad), F32)],
        axis=-1).reshape(DEPTH, 1, ROUTER_LANES)
    moe_in = moe_w_in.reshape(DEPTH, N_EXPERTS * d, 2 * EXPERT_HIDDEN)
    moe_down = moe_w_down.reshape(DEPTH, N_EXPERTS * EXPERT_HIDDEN, d)

    mix_in = {0: w_in_a[0].astype(BF16)}
    mix_out = {0: w_out_a[0].astype(BF16)}
    later = [(w_in_a, l) for l in range(1, N_A_LAYERS)] + [(w_out_a, l) for l in range(1, N_A_LAYERS)]
    later += [(w.reshape(1, -1, w.shape[-1]), 0) for w in (w_in_b, w_out_b, w_kv)]

    sb_k = sb_v = None
    for l in range(DEPTH):
        g_mix = mix_norm_g[l].reshape(1, d)
        out_proj = None
        if l < N_A_LAYERS:
            side = [(moe_in, l), (moe_down, l)] + (later if l == 0 else [])
            xt, e_in, e_down, *copies = _mixer_a(
                xt, g_mix, mix_in[l], conv_w, conv_ln_g[l].reshape(1, CONV_CH),
                conv_ln_b[l].reshape(1, CONV_CH), mem_qg[l], mk_big, mv_big, mix_out[l], side, l, tm)
            if l == 0:
                for k, l_a in enumerate(range(1, N_A_LAYERS)):
                    mix_in[l_a] = copies[k]
                    mix_out[l_a] = copies[N_A_LAYERS - 1 + k]
                w_in_b16, w_out_b16, w_kv16 = copies[2 * (N_A_LAYERS - 1):]
                w_in_b16 = w_in_b16.reshape(w_in_b.shape)
                w_out_b16 = w_out_b16.reshape(w_out_b.shape)
                shared_kv = (kv_norm_g.reshape(1, d), w_kv16, sb_k_norm_g.reshape(1, SB_HEAD_DIM))
        else:
            j = l - N_A_LAYERS
            outs = _mixer_b_pre(xt, g_mix, w_in_b16, sb_q_norm_g[j].reshape(1, SB_HEAD_DIM),
                                mem_qg[l], mk_big, mv_big, moe_in, moe_down, j, l, tm_wide,
                                kv=shared_kv if j == 0 else None)
            q, mo, e_in, e_down = outs[:4]
            if j == 0:
                sb_k, sb_v = outs[4:]
            out_proj = (_sb_attention(q, sb_k, sb_v), mo, w_out_b16, j)
        xt = _moe(xt, ffn_norm_g[l].reshape(1, d), w_router, b_router,
                  e_in.reshape(N_EXPERTS, d, 2 * EXPERT_HIDDEN), e_down, l, tm_wide, out_proj=out_proj)
    return xt.reshape(b, t, d)
```

```python
import functools

import jax
import jax.numpy as jnp
from jax import lax
from jax.experimental import pallas as pl
from jax.experimental.pallas import tpu as pltpu

F32 = jnp.float32
BF16 = jnp.bfloat16

D_MODEL = 1024
DEPTH = 4
N_A_LAYERS = DEPTH // 2
MEM_TOKENS = 256
MEM_HEADS = 4
MEM_HEAD_DIM = 64
MEM_WIDTH = MEM_HEADS * MEM_HEAD_DIM
CONV_CH = D_MODEL - MEM_WIDTH
CONV_WIDTH = 31
SB_HEADS = 4
SB_HEAD_DIM = 128
SB_WIDTH = SB_HEADS * SB_HEAD_DIM
N_GROUPS = 4
EXPERTS_PER_GROUP = 4
N_EXPERTS = N_GROUPS * EXPERTS_PER_GROUP
EXPERT_HIDDEN = D_MODEL // 8
EPS = 1e-6

LANES = 128
SUBLANES = 8
CONV_HALO = 32
CONV_ROWS = 128
CONV_PAD = 2 * SUBLANES
SB_BLOCK = 128
SB_GROUP = 32
SB_UNROLLED_BLOCKS = 3
LOG2_E = 1.4426950408889634
SB_ZERO_LOG2 = -106.0 * LOG2_E
VMEM_LIMIT = 56 * 1024 * 1024
CONV_TOKEN_TILE = 512
WIDE_TOKEN_TILE = 1024


def _dot(a, b):
    return jnp.dot(a, b, preferred_element_type=F32)


def _split(a):
    hi = a.astype(BF16)
    lo = (a - hi.astype(F32)).astype(BF16)
    return hi, lo


def _dot_exact_rhs(a, b):
    hi, lo = _split(a)
    return _dot(hi, b) + _dot(lo, b)


def _dot3(a, b):
    ah, al = _split(a)
    bh, bl = _split(b)
    n = b.shape[1]
    both = _dot(ah, jnp.concatenate([bh, bl], axis=1))
    return both[:, :n] + (both[:, n:] + _dot(al, bh))


def _rms(x, g):
    ms = jnp.mean(x * x, axis=-1, keepdims=True)
    return x * lax.rsqrt(ms + EPS) * g


def _sigmoid(x):
    return 0.5 * jnp.tanh(0.5 * x) + 0.5


def _silu(x):
    h = 0.5 * x
    return h * jnp.tanh(h) + h


def _full_spec(shape):
    return pl.BlockSpec(shape, lambda *_: (0,) * len(shape), pipeline_mode=pl.Buffered(1))


def _layer_spec(stacked, layer):
    rest = stacked.shape[1:]
    return pl.BlockSpec((None,) + rest, lambda *_: (layer,) + (0,) * len(rest),
                        pipeline_mode=pl.Buffered(1))


def _bf16_copy_specs(stacked, layer, n):
    rows, cols = stacked.shape[1:]
    step_rows = rows // n
    assert step_rows * n == rows
    block = lambda i: jnp.minimum(i, n - 1)
    return (pl.BlockSpec((None, step_rows, cols), lambda i: (layer, block(i), 0)),
            pl.BlockSpec((step_rows, cols), lambda i: (block(i), 0)),
            jax.ShapeDtypeStruct((rows, cols), BF16))


def _head_block_ones(n, head_dim):
    r = lax.broadcasted_iota(jnp.int32, (n, n), 0) // head_dim
    c = lax.broadcasted_iota(jnp.int32, (n, n), 1) // head_dim
    return (r == c).astype(BF16)


def _memkv_kernel(mem_ref, g_ref, w_ref, kg_ref, mk_ref, mv_ref):
    mem_n = _rms(mem_ref[...], g_ref[...]).astype(BF16)
    kv = _dot(mem_n, w_ref[...])
    mk = kv[:, :MEM_WIDTH]
    mv = kv[:, MEM_WIDTH:]
    ss = _dot_exact_rhs(mk * mk, _head_block_ones(MEM_WIDTH, MEM_HEAD_DIM))
    mk = mk * lax.rsqrt(ss * (1.0 / MEM_HEAD_DIM) + EPS) * (kg_ref[...] * MEM_HEAD_DIM ** -0.5)
    mk_t = mk.T
    mk_ref[...] = jnp.zeros(mk_ref.shape, mk_ref.dtype)
    lane_head = lax.broadcasted_iota(jnp.int32, mv.shape, 1) // MEM_HEAD_DIM
    for h in range(MEM_HEADS):
        rows = slice(h * MEM_HEAD_DIM, (h + 1) * MEM_HEAD_DIM)
        mk_ref[rows, h * MEM_TOKENS:(h + 1) * MEM_TOKENS] = mk_t[rows, :].astype(BF16)
        mv_ref[h * MEM_TOKENS:(h + 1) * MEM_TOKENS, :] = jnp.where(lane_head == h, mv, 0.0).astype(BF16)


def _memkv(mem, mem_norm_g, w_mem_kv, mem_k_norm_g):
    kg = jnp.tile(mem_k_norm_g, (1, MEM_HEADS)).reshape(DEPTH, 1, MEM_WIDTH)
    return pl.pallas_call(
        _memkv_kernel,
        grid=(DEPTH,),
        in_specs=[
            pl.BlockSpec((MEM_TOKENS, D_MODEL), lambda l: (0, 0)),
            pl.BlockSpec((1, D_MODEL), lambda l: (0, 0)),
            pl.BlockSpec((None, D_MODEL, 2 * MEM_WIDTH), lambda l: (l, 0, 0)),
            pl.BlockSpec((None, 1, MEM_WIDTH), lambda l: (l, 0, 0)),
        ],
        out_specs=[
            pl.BlockSpec((None, MEM_WIDTH, MEM_HEADS * MEM_TOKENS), lambda l: (l, 0, 0)),
            pl.BlockSpec((None, MEM_HEADS * MEM_TOKENS, MEM_WIDTH), lambda l: (l, 0, 0)),
        ],
        out_shape=[
            jax.ShapeDtypeStruct((DEPTH, MEM_WIDTH, MEM_HEADS * MEM_TOKENS), BF16),
            jax.ShapeDtypeStruct((DEPTH, MEM_HEADS * MEM_TOKENS, MEM_WIDTH), BF16),
        ],
        name="memkv",
    )(mem, mem_norm_g.reshape(1, D_MODEL), w_mem_kv.astype(BF16), kg)


def _mem_attention(q, qg, mk_big, mv_big):
    ss = _dot_exact_rhs(q * q, _head_block_ones(MEM_WIDTH, MEM_HEAD_DIM))
    qn = q * lax.rsqrt(ss * (1.0 / MEM_HEAD_DIM) + EPS) * qg
    s = _dot(qn.astype(BF16), mk_big)
    probs = []
    for h in range(MEM_HEADS):
        sh = s[:, h * MEM_TOKENS:(h + 1) * MEM_TOKENS]
        e = jnp.exp(sh - jnp.max(sh, axis=-1, keepdims=True))
        probs.append((e * (1.0 / jnp.sum(e, axis=-1, keepdims=True))).astype(BF16))
    return _dot(jnp.concatenate(probs, axis=1), mv_big)


def _mixer_a_kernel(*refs, n_side):
    x_ref, g_ref, win_ref, cw_ref, lng_ref, lnb_ref, qg_ref, mk_ref, mv_ref, wout_ref = refs[:10]
    side_in, o_ref, side_out = refs[10:10 + n_side], refs[10 + n_side], refs[11 + n_side:11 + 2 * n_side]
    cbuf_ref, conv_ref, xprev_ref, moprev_ref = refs[11 + 2 * n_side:]
    tm = x_ref.shape[0]
    for src, dst in zip(side_in, side_out):
        dst[...] = src[...].astype(BF16)

    @pl.when(pl.program_id(0) == 0)
    def _():
        for ref in (cbuf_ref, xprev_ref, moprev_ref):
            ref[...] = jnp.zeros(ref.shape, ref.dtype)

    x = x_ref[...]
    h = _rms(x, g_ref[...]).astype(BF16)
    u = _dot(h, win_ref[...])

    base = CONV_HALO - (CONV_WIDTH - 1)
    win = CONV_ROWS + SUBLANES
    for r0 in range(0, tm, CONV_ROWS):
        for c0 in range(0, CONV_CH, LANES):
            cols = slice(c0, c0 + LANES)
            acc = None
            for shift in range(SUBLANES):
                part = None
                for k in range(CONV_WIDTH):
                    if (base + k) % SUBLANES != shift:
                        continue
                    start = r0 + (base + k - shift)
                    term = cw_ref[k:k + 1, cols] * cbuf_ref[start:start + win, cols]
                    part = term if part is None else part + term
                part = part[shift:shift + CONV_ROWS, :]
                acc = part if acc is None else acc + part
            conv_ref[r0:r0 + CONV_ROWS, cols] = acc

    cbuf_ref[0:CONV_HALO, :] = cbuf_ref[tm:tm + CONV_HALO, :]
    cbuf_ref[CONV_HALO:CONV_HALO + tm, :] = u[:, :CONV_CH] * _sigmoid(u[:, CONV_CH:2 * CONV_CH])

    c = conv_ref[...]
    xc = c - jnp.mean(c, axis=-1, keepdims=True)
    var = jnp.mean(xc * xc, axis=-1, keepdims=True)
    y = xc * lax.rsqrt(var + EPS) * lng_ref[...] + lnb_ref[...]
    mixed = _silu(y).astype(BF16)
    o_ref[...] = (xprev_ref[...] + _dot(mixed, wout_ref[0:CONV_CH, :])
                  + _dot(moprev_ref[...], wout_ref[CONV_CH:, :]))

    xprev_ref[...] = x
    moprev_ref[...] = _mem_attention(u[:, 2 * CONV_CH:], qg_ref[...], mk_ref[...],
                                     mv_ref[...]).astype(BF16)


def _mixer_a(x, g, w_in, conv_w, ln_g, ln_b, qg, mk_big, mv_big, w_out, side_casts, layer, tm):
    t = x.shape[0]
    n = t // tm
    full = _full_spec
    side = [_bf16_copy_specs(stacked, side_layer, n) for stacked, side_layer in side_casts]
    return pl.pallas_call(
        functools.partial(_mixer_a_kernel, n_side=len(side)),
        grid=(n + 1,),
        in_specs=[
            pl.BlockSpec((tm, D_MODEL), lambda i: (jnp.minimum(i, n - 1), 0)),
            full((1, D_MODEL)),
            full(w_in.shape),
            _layer_spec(conv_w, layer),
            full((1, CONV_CH)),
            full((1, CONV_CH)),
            full((1, MEM_WIDTH)),
            _layer_spec(mk_big, layer),
            _layer_spec(mv_big, layer),
            full(w_out.shape),
            *[s[0] for s in side],
        ],
        out_specs=[pl.BlockSpec((tm, D_MODEL), lambda i: (jnp.maximum(i - 1, 0), 0)),
                   *[s[1] for s in side]],
        out_shape=[jax.ShapeDtypeStruct((t, D_MODEL), F32), *[s[2] for s in side]],
        scratch_shapes=[
            pltpu.VMEM((tm + CONV_HALO + CONV_PAD, CONV_CH), F32),
            pltpu.VMEM((tm, CONV_CH), F32),
            pltpu.VMEM((tm, D_MODEL), F32),
            pltpu.VMEM((tm, MEM_WIDTH), BF16),
        ],
        compiler_params=pltpu.CompilerParams(
            dimension_semantics=("arbitrary",), vmem_limit_bytes=VMEM_LIMIT),
        name="mixer_a",
    )(x, g, w_in, conv_w, ln_g, ln_b, qg, mk_big, mv_big, w_out, *[w for w, _ in side_casts])


ROUTER_LANES = LANES
EXPERT_LANE0 = N_GROUPS


def _route(logits):
    neg = jnp.float32(-jnp.inf)
    lane = lax.broadcasted_iota(jnp.int32, logits.shape, 1)
    lane_f = lane.astype(F32)
    is_g = lane < N_GROUPS
    gl = jnp.where(is_g, logits, neg)
    gmax = jnp.max(gl, axis=-1, keepdims=True)
    gidx = jnp.min(jnp.where(gl == gmax, lane_f, float(ROUTER_LANES)), axis=-1, keepdims=True)
    gsum = jnp.sum(jnp.where(is_g, jnp.exp(gl - gmax), 0.0), axis=-1, keepdims=True)
    g_gate = 1.0 / gsum
    lane_group = ((lane - EXPERT_LANE0) // EXPERTS_PER_GROUP).astype(F32)
    sel = (lane >= EXPERT_LANE0) & (lane < EXPERT_LANE0 + N_EXPERTS) & (lane_group == gidx)
    sl = jnp.where(sel, logits, neg)
    m1 = jnp.max(sl, axis=-1, keepdims=True)
    i1 = jnp.min(jnp.where(sl == m1, lane_f, float(ROUTER_LANES)), axis=-1, keepdims=True)
    sl2 = jnp.where(lane_f == i1, neg, sl)
    m2 = jnp.max(sl2, axis=-1, keepdims=True)
    i2 = jnp.min(jnp.where(sl2 == m2, lane_f, float(ROUTER_LANES)), axis=-1, keepdims=True)
    e2 = jnp.exp(m2 - m1)
    w1 = g_gate / (1.0 + e2)
    w2 = w1 * e2
    return jnp.where(lane_f == i1, w1, jnp.where(lane_f == i2, w2, 0.0))


def _moe_kernel(*refs, with_out_proj):
    if with_out_proj:
        x_ref, sb_ref, mo_ref, wout_ref, g_ref, wr_ref, br_ref, win_ref, wdn_ref, o_ref, act_ref = refs
        x = (x_ref[...] + _dot(sb_ref[...], wout_ref[0:SB_WIDTH, :])
             + _dot(mo_ref[...], wout_ref[SB_WIDTH:, :]))
    else:
        x_ref, g_ref, wr_ref, br_ref, win_ref, wdn_ref, o_ref, act_ref = refs
        x = x_ref[...]
    hf = _rms(x, g_ref[...])
    hb = hf.astype(BF16)
    gate = _route(_dot3(hf, wr_ref[...]) + br_ref[...])
    for e in range(N_EXPERTS):
        hu = _dot(hb, win_ref[e])
        a = hu[:, :EXPERT_HIDDEN]
        act = _silu(a) * hu[:, EXPERT_HIDDEN:] * gate[:, EXPERT_LANE0 + e:EXPERT_LANE0 + e + 1]
        act_ref[:, e * EXPERT_HIDDEN:(e + 1) * EXPERT_HIDDEN] = act.astype(BF16)
    o_ref[...] = x + _dot(act_ref[...], wdn_ref[...])


def _moe(x, g, w_router, b_router, w_in, w_down, layer, tm, out_proj=None):
    t = x.shape[0]
    tokens = lambda width: pl.BlockSpec((tm, width), lambda i: (i, 0))
    args, in_specs = [x], [tokens(D_MODEL)]
    if out_proj is not None:
        sb, mo, w_out, j = out_proj
        args += [sb, mo, w_out]
        in_specs += [tokens(SB_WIDTH), tokens(MEM_WIDTH), _layer_spec(w_out, j)]
    args += [g, w_router, b_router, w_in, w_down]
    in_specs += [_full_spec((1, D_MODEL)), _layer_spec(w_router, layer), _layer_spec(b_router, layer),
                 _full_spec(w_in.shape), _full_spec(w_down.shape)]
    return pl.pallas_call(
        functools.partial(_moe_kernel, with_out_proj=out_proj is not None),
        grid=(t // tm,),
        in_specs=in_specs,
        out_specs=tokens(D_MODEL),
        out_shape=jax.ShapeDtypeStruct((t, D_MODEL), F32),
        scratch_shapes=[pltpu.VMEM((tm, N_EXPERTS * EXPERT_HIDDEN), BF16)],
        compiler_params=pltpu.CompilerParams(
            dimension_semantics=("arbitrary",), vmem_limit_bytes=VMEM_LIMIT),
        name="moe",
    )(*args)


def _sb_queries(u, sqg_ref, q_ref):
    sqg = sqg_ref[...] * (SB_HEAD_DIM ** -0.5 * LOG2_E)
    for hd in range(SB_HEADS):
        cols = slice(hd * SB_HEAD_DIM, (hd + 1) * SB_HEAD_DIM)
        q_ref[:, cols] = _rms(u[:, cols], sqg).astype(BF16)


def _mixer_b_pre_kernel(*refs, with_kv):
    if with_kv:
        (x_ref, g_ref, win_ref, sqg_ref, qg_ref, mk_ref, mv_ref, ein_ref, edn_ref,
         kvg_ref, wkv_ref, kg_ref, q_ref, mo_ref, ein_bf_ref, edn_bf_ref, k_ref, v_ref) = refs
    else:
        (x_ref, g_ref, win_ref, sqg_ref, qg_ref, mk_ref, mv_ref, ein_ref, edn_ref,
         q_ref, mo_ref, ein_bf_ref, edn_bf_ref) = refs
    ein_bf_ref[...] = ein_ref[...].astype(BF16)
    edn_bf_ref[...] = edn_ref[...].astype(BF16)
    x = x_ref[...]
    unit = x * lax.rsqrt(jnp.mean(x * x, axis=-1, keepdims=True) + EPS)
    u = _dot((unit * g_ref[...]).astype(BF16), win_ref[...])
    _sb_queries(u, sqg_ref, q_ref)
    mo_ref[...] = _mem_attention(u[:, SB_WIDTH:], qg_ref[...], mk_ref[...], mv_ref[...]).astype(BF16)
    if with_kv:
        kv = _dot((unit * kvg_ref[...]).astype(BF16), wkv_ref[...])
        for h in range(SB_HEADS):
            cols = slice(h * SB_HEAD_DIM, (h + 1) * SB_HEAD_DIM)
            k_ref[:, cols] = _rms(kv[:, cols], kg_ref[...]).astype(BF16)
        v_ref[...] = kv[:, SB_WIDTH:].astype(BF16)


def _mixer_b_pre(x, g, w_in, sqg, qg, mk_big, mv_big, moe_in, moe_down, j, layer, tm, kv=None):
    t = x.shape[0]
    full = _full_spec
    tokens = lambda width: pl.BlockSpec((tm, width), lambda i: (i, 0))
    ein_in, ein_out, ein_shape = _bf16_copy_specs(moe_in, layer, t // tm)
    edn_in, edn_out, edn_shape = _bf16_copy_specs(moe_down, layer, t // tm)
    args = [x, g, w_in, sqg, qg, mk_big, mv_big, moe_in, moe_down]
    in_specs = [tokens(D_MODEL), full((1, D_MODEL)), _layer_spec(w_in, j), full((1, SB_HEAD_DIM)),
                full((1, MEM_WIDTH)), _layer_spec(mk_big, layer), _layer_spec(mv_big, layer),
                ein_in, edn_in]
    out_specs = [tokens(SB_WIDTH), tokens(MEM_WIDTH), ein_out, edn_out]
    out_shape = [jax.ShapeDtypeStruct((t, SB_WIDTH), BF16), jax.ShapeDtypeStruct((t, MEM_WIDTH), BF16),
                 ein_shape, edn_shape]
    if kv is not None:
        kvg, w_kv, kg = kv
        args += [kvg, w_kv, kg]
        in_specs += [full((1, D_MODEL)), full(w_kv.shape), full((1, SB_HEAD_DIM))]
        out_specs += [tokens(SB_WIDTH)] * 2
        out_shape += [jax.ShapeDtypeStruct((t, SB_WIDTH), BF16)] * 2
    return pl.pallas_call(
        functools.partial(_mixer_b_pre_kernel, with_kv=kv is not None),
        grid=(t // tm,),
        in_specs=in_specs,
        out_specs=out_specs,
        out_shape=out_shape,
        compiler_params=pltpu.CompilerParams(
            dimension_semantics=("arbitrary",), vmem_limit_bytes=VMEM_LIMIT),
        name="mixer_b_pre",
    )(*args)


def _sb_kernel(q_ref, k_ref, v_ref, o_ref):
    first_block = pl.program_id(1) * SB_GROUP
    row = lax.broadcasted_iota(jnp.int32, (SB_BLOCK, SB_BLOCK), 0)
    col = lax.broadcasted_iota(jnp.int32, (SB_BLOCK, SB_BLOCK), 1)
    causal = col < row
    suffix_ones = (row >= col).astype(BF16)

    def log2_one_minus_sigmoid(z):
        return jnp.minimum(-z, 0.0) - jnp.log(1.0 + jnp.exp2(-jnp.abs(z))) * LOG2_E

    def earlier_key_block(q, kb, carry, acc):
        start = pl.multiple_of(kb * SB_BLOCK, SB_BLOCK)
        k = k_ref[pl.ds(start, SB_BLOCK), :]
        v = v_ref[pl.ds(start, SB_BLOCK), :]
        z = lax.dot_general(q, k, (((1,), (1,)), ((), ())), preferred_element_type=F32)
        within = _dot(log2_one_minus_sigmoid(z).astype(BF16), suffix_ones)
        w = jnp.exp2(z + within + carry)
        acc = acc + _dot(w.astype(BF16), v)
        return carry + within[:, 0:1], acc

    def block_rows(g, n=1):
        return slice(g * SB_BLOCK, (g + n) * SB_BLOCK)

    def run(first_step):
        offsets = range(0 if first_step else 1 - SB_UNROLLED_BLOCKS, SB_GROUP)
        users = {off: range(max(off, 0), min(off + SB_UNROLLED_BLOCKS, SB_GROUP)) for off in offsets}
        tiles = [(g, g - off) for off in offsets for g in users[off]]

        z = {}
        for off in offsets:
            start = pl.multiple_of((first_block + off) * SB_BLOCK, SB_BLOCK)
            k = k_ref[pl.ds(start, SB_BLOCK), :]
            gs = users[off]
            zz = lax.dot_general(q_ref[block_rows(gs[0], len(gs)), :], k, (((1,), (1,)), ((), ())),
                                 preferred_element_type=F32)
            for n, g in enumerate(gs):
                z[g, g - off] = zz[block_rows(n), :]

        parts = []
        for g, back in tiles:
            lf = log2_one_minus_sigmoid(z[g, back])
            if back == 0:
                lf = jnp.where(causal, lf, 0.0)
            parts.append(lf.astype(BF16))
        within_all = _dot(jnp.concatenate(parts, axis=0), suffix_ones)
        within = {t: within_all[block_rows(n), :] for n, t in enumerate(tiles)}

        carries, w = [], {}
        for g in range(SB_GROUP):
            carry = jnp.zeros((SB_BLOCK, 1), F32)
            for back in range(SB_UNROLLED_BLOCKS):
                if (g, back) not in within:
                    continue
                wt = jnp.exp2(z[g, back] + within[g, back] + carry)
                if back == 0:
                    wt = jnp.where(causal, wt, 0.0)
                w[g, back] = wt.astype(BF16)
                carry = carry + within[g, back][:, 0:1]
            carries.append(carry)

        accs = [jnp.zeros((SB_BLOCK, SB_HEAD_DIM), F32) for _ in range(SB_GROUP)]
        for off in offsets:
            start = pl.multiple_of((first_block + off) * SB_BLOCK, SB_BLOCK)
            v = v_ref[pl.ds(start, SB_BLOCK), :]
            gs = users[off]
            pv = _dot(jnp.concatenate([w[g, g - off] for g in gs], axis=0), v)
            for n, g in enumerate(gs):
                accs[g] = accs[g] + pv[block_rows(n), :]
        for g in range(SB_GROUP):
            o_ref[block_rows(g), :] = accs[g].astype(o_ref.dtype)

        worst = functools.reduce(jnp.maximum, carries)

        @pl.when(jnp.max(worst) > SB_ZERO_LOG2)
        def _():
            for g in range(SB_GROUP):
                q = q_ref[block_rows(g), :]

                def cond(state):
                    kb, carry, _ = state
                    return jnp.logical_and(kb >= 0, jnp.max(carry) > SB_ZERO_LOG2)

                def body(state):
                    kb, carry, acc = state
                    carry, acc = earlier_key_block(q, kb, carry, acc)
                    return kb - 1, carry, acc

                start = first_block + g - SB_UNROLLED_BLOCKS
                _, _, acc = lax.while_loop(cond, body, (start, carries[g], accs[g]))
                o_ref[block_rows(g), :] = acc.astype(o_ref.dtype)

    assert SB_GROUP >= SB_UNROLLED_BLOCKS - 1
    pl.when(first_block == 0)(lambda: run(True))
    pl.when(first_block > 0)(lambda: run(False))


def _sb_attention(q, k, v):
    t = q.shape[0]
    tq = SB_GROUP * SB_BLOCK
    return pl.pallas_call(
        _sb_kernel,
        grid=(SB_HEADS, t // tq),
        in_specs=[
            pl.BlockSpec((tq, SB_HEAD_DIM), lambda h, i: (i, h)),
            pl.BlockSpec((t, SB_HEAD_DIM), lambda h, i: (0, h)),
            pl.BlockSpec((t, SB_HEAD_DIM), lambda h, i: (0, h)),
        ],
        out_specs=pl.BlockSpec((tq, SB_HEAD_DIM), lambda h, i: (i, h)),
        out_shape=jax.ShapeDtypeStruct((t, SB_WIDTH), BF16),
        compiler_params=pltpu.CompilerParams(
            dimension_semantics=("arbitrary", "arbitrary"), vmem_limit_bytes=VMEM_LIMIT),
        name="sb_attention",
    )(q, k, v)


def _token_tile(t, tm):
    while t % tm:
        tm //= 2
    return tm


def kernel(x, mem, mem_norm_g, mix_norm_g, ffn_norm_g, w_in_a, conv_w, conv_ln_g, conv_ln_b, w_out_a, w_in_b, sb_q_norm_g, w_out_b, kv_norm_g, w_kv, sb_k_norm_g, w_mem_kv, mem_q_norm_g, mem_k_norm_g, router_g_w, router_g_b, router_e_w, router_e_b, moe_w_in, moe_w_down):
    b, t, d = x.shape
    assert b == 1 and d == D_MODEL and t % (SB_GROUP * SB_BLOCK) == 0
    tm = _token_tile(t, CONV_TOKEN_TILE)
    tm_wide = _token_tile(t, WIDE_TOKEN_TILE)
    xt = x.reshape(t, d)

    mk_big, mv_big = _memkv(mem.reshape(MEM_TOKENS, d), mem_norm_g, w_mem_kv, mem_k_norm_g)
    mem_qg = jnp.tile(mem_q_norm_g, (1, MEM_HEADS)).reshape(DEPTH, 1, MEM_WIDTH)

    pad = ROUTER_LANES - N_GROUPS - N_EXPERTS
    w_router = jnp.concatenate(
        [router_g_w, router_e_w.transpose(0, 2, 1, 3).reshape(DEPTH, d, N_EXPERTS),
         jnp.zeros((DEPTH, d, pad), F32)], axis=-1)
    b_router = jnp.concatenate(
        [router_g_b, router_e_b.reshape(DEPTH, N_EXPERTS), jnp.zeros((DEPTH, pad), F32)],
        axis=-1).reshape(DEPTH, 1, ROUTER_LANES)
    moe_in = moe_w_in.reshape(DEPTH, N_EXPERTS * d, 2 * EXPERT_HIDDEN)
    moe_down = moe_w_down.reshape(DEPTH, N_EXPERTS * EXPERT_HIDDEN, d)

    mix_in = {0: w_in_a[0].astype(BF16)}
    mix_out = {0: w_out_a[0].astype(BF16)}
    later = [(w_in_a, l) for l in range(1, N_A_LAYERS)] + [(w_out_a, l) for l in range(1, N_A_LAYERS)]
    later += [(w.reshape(1, -1, w.shape[-1]), 0) for w in (w_in_b, w_out_b, w_kv)]

    sb_k = sb_v = None
    for l in range(DEPTH):
        g_mix = mix_norm_g[l].reshape(1, d)
        out_proj = None
        if l < N_A_LAYERS:
            side = [(moe_in, l), (moe_down, l)] + (later if l == 0 else [])
            xt, e_in, e_down, *copies = _mixer_a(
                xt, g_mix, mix_in[l], conv_w, conv_ln_g[l].reshape(1, CONV_CH),
                conv_ln_b[l].reshape(1, CONV_CH), mem_qg[l], mk_big, mv_big, mix_out[l], side, l, tm)
            if l == 0:
                for k, l_a in enumerate(range(1, N_A_LAYERS)):
                    mix_in[l_a] = copies[k]
                    mix_out[l_a] = copies[N_A_LAYERS - 1 + k]
                w_in_b16, w_out_b16, w_kv16 = copies[2 * (N_A_LAYERS - 1):]
                w_in_b16 = w_in_b16.reshape(w_in_b.shape)
                w_out_b16 = w_out_b16.reshape(w_out_b.shape)
                shared_kv = (kv_norm_g.reshape(1, d), w_kv16, sb_k_norm_g.reshape(1, SB_HEAD_DIM))
        else:
            j = l - N_A_LAYERS
            outs = _mixer_b_pre(xt, g_mix, w_in_b16, sb_q_norm_g[j].reshape(1, SB_HEAD_DIM),
                                mem_qg[l], mk_big, mv_big, moe_in, moe_down, j, l, tm_wide,
                                kv=shared_kv if j == 0 else None)
            q, mo, e_in, e_down = outs[:4]
            if j == 0:
                sb_k, sb_v = outs[4:]
            out_proj = (_sb_attention(q, sb_k, sb_v), mo, w_out_b16, j)
        xt = _moe(xt, ffn_norm_g[l].reshape(1, d), w_router, b_router,
                  e_in.reshape(N_EXPERTS, d, 2 * EXPERT_HIDDEN), e_down, l, tm_wide, out_proj=out_proj)
    return xt.reshape(b, t, d)
```
